```python
import math
import jax
import jax.numpy as jnp
from jax import lax
import numpy as np

D_MODEL = 1024
BATCH = 4
SEQ = 8192
DEPTH = 2
DEC_BATCH = 16
DEC_SEQ = 64
PAST_LEN = 4096

CHUNK = 64
N_HEADS = 4
HEAD_DIM = 64
GROUP_WIDTH = N_HEADS * HEAD_DIM
D_MIX = 4 * GROUP_WIDTH
CONV_WIDTH = 4
RWKV_DECAY_LORA = 64
RWKV_AAA_LORA = 64
RWKV_GATE_LORA = 128
RWKV_IN = 3 * GROUP_WIDTH + RWKV_DECAY_LORA + RWKV_AAA_LORA + RWKV_GATE_LORA
RWKV_SPLITS = (GROUP_WIDTH, 2 * GROUP_WIDTH, 3 * GROUP_WIDTH, 3 * GROUP_WIDTH + RWKV_DECAY_LORA, 3 * GROUP_WIDTH + RWKV_DECAY_LORA + RWKV_AAA_LORA)
RWKV_W_SCALE = 0.606531
RWKV_LN_EPS = 64e-5
LRU_BLOCKS = 4
LRU_BD = GROUP_WIDTH // LRU_BLOCKS
LRU_C = 8.0
ROPE_BASE = 10000.0
N_GROUPS = 4
EXPERTS_PER_GROUP = 4
N_EXPERTS = N_GROUPS * EXPERTS_PER_GROUP
TOP_K_IN_GROUP = 2
D_EXPERT = 512
IN_SIZES = (3 * GROUP_WIDTH, GROUP_WIDTH, N_HEADS, N_HEADS, RWKV_IN, GROUP_WIDTH, GROUP_WIDTH, 3 * GROUP_WIDTH, GROUP_WIDTH)
N_IN = 3 * GROUP_WIDTH + GROUP_WIDTH + 2 * N_HEADS + RWKV_IN + 2 * GROUP_WIDTH + 4 * GROUP_WIDTH
STATE_NAMES = ("gdn", "gdn_conv", "rwkv", "rwkv_shift", "lru", "lru_conv", "ret")

kernel_name = "hymba_style_gdn_rwkv7_rglru_retention_hmoe_step"


def rmsnorm(x, w, eps=1e-6):
    xf = x.astype(jnp.float32)
    y = xf * lax.rsqrt(jnp.mean(xf * xf, axis=-1, keepdims=True) + eps)
    return (y * w.astype(jnp.float32)).astype(x.dtype)


def l2norm(x, eps=1e-6):
    return x * lax.rsqrt(jnp.sum(x * x, axis=-1, keepdims=True) + eps)


def groupnorm_heads(x, w, b, eps):
    B, L = x.shape[:2]
    mu = jnp.mean(x, axis=-1, keepdims=True)
    var = jnp.mean(jnp.square(x - mu), axis=-1, keepdims=True)
    y = ((x - mu) * lax.rsqrt(var + eps)).reshape(B, L, -1)
    return y * w + b


def causal_conv(x, buf, w):
    K = w.shape[0]
    L = x.shape[1]
    xp = jnp.concatenate([buf, x], axis=1)
    y = xp[:, 0:L] * w[0]
    for j in range(1, K):
        y = y + xp[:, j:j + L] * w[j]
    return y, xp[:, L:]


def rotary(x, pos):
    half = x.shape[-1] // 2
    inv = ROPE_BASE ** (-jnp.arange(half, dtype=jnp.float32) / half)
    ang = pos.astype(jnp.float32)[:, None] * inv
    cos = jnp.cos(ang)[:, None, :]
    sin = jnp.sin(ang)[:, None, :]
    x1, x2 = x[..., :half], x[..., half:]
    return jnp.concatenate([x1 * cos - x2 * sin, x2 * cos + x1 * sin], axis=-1)


def to_chunks(t, c):
    B, L = t.shape[:2]
    n = -(-L // c)
    t = jnp.pad(t, [(0, 0), (0, n * c - L)] + [(0, 0)] * (t.ndim - 2))
    t = t.reshape((B, n, c) + t.shape[2:])
    return jnp.moveaxis(jnp.moveaxis(t, 1, 0), 2, 3)


def from_chunks(o, L):
    n, B, H, c, D = o.shape
    return jnp.transpose(o, (1, 0, 3, 2, 4)).reshape(B, n * c, H, D)[:, :L]


def decay_masks(G, c):
    incl = jnp.tril(jnp.ones((c, c), dtype=bool))
    strict = jnp.tril(jnp.ones((c, c), dtype=bool), k=-1)
    diff = G[..., :, None] - G[..., None, :]
    return jnp.exp(jnp.where(incl, diff, -jnp.inf)), strict


def gated_delta_rule(q, k, v, g, beta, S0):
    L = q.shape[1]
    Dk = q.shape[-1]
    c = min(CHUNK, L)
    qc, kc, vc = to_chunks(q, c), to_chunks(k, c), to_chunks(v, c)
    gc, bc = to_chunks(g, c), to_chunks(beta, c)
    G = jnp.cumsum(gc, axis=-1)
    decay, strict = decay_masks(G, c)
    kb = kc * bc[..., None]
    A = jnp.where(strict, jnp.einsum('nbhid,nbhjd->nbhij', kb, kc) * decay, 0.0)
    eye = jnp.eye(c, dtype=jnp.float32)
    rhs = jnp.concatenate([kb * jnp.exp(G)[..., None], vc * bc[..., None]], axis=-1)
    sol = lax.linalg.triangular_solve(eye + A, rhs, left_side=True, lower=True, unit_diagonal=True)
    w, u = sol[..., :Dk], sol[..., Dk:]
    qk = jnp.einsum('nbhid,nbhjd->nbhij', qc, kc) * decay
    qg = qc * jnp.exp(G)[..., None]
    kg = kc * jnp.exp(G[..., -1:] - G)[..., None]
    gl = jnp.exp(G[..., -1])

    def step(S, xs):
        w_, u_, qg_, qk_, kg_, gl_ = xs
        vn = u_ - jnp.einsum('bhik,bhkv->bhiv', w_, S)
        o = jnp.einsum('bhik,bhkv->bhiv', qg_, S) + jnp.einsum('bhij,bhjv->bhiv', qk_, vn)
        S = S * gl_[..., None, None] + jnp.einsum('bhik,bhiv->bhkv', kg_, vn)
        return S, o

    S, o = lax.scan(step, S0, (w, u, qg, qk, kg, gl))
    return from_chunks(o, L), S


def retention_chunked(q, k, v, g, R0):
    L = q.shape[1]
    c = min(CHUNK, L)
    qc, kc, vc, gc = to_chunks(q, c), to_chunks(k, c), to_chunks(v, c), to_chunks(g, c)
    G = jnp.cumsum(gc, axis=-1)
    decay, _ = decay_masks(G, c)
    qk = jnp.einsum('nbhid,nbhjd->nbhij', qc, kc) * decay
    qg = qc * jnp.exp(G)[..., None]
    kg = kc * jnp.exp(G[..., -1:] - G)[..., None]
    gl = jnp.exp(G[..., -1])

    def step(R, xs):
        v_, qg_, qk_, kg_, gl_ = xs
        o = jnp.einsum('bhik,bhkv->bhiv', qg_, R) + jnp.einsum('bhij,bhjv->bhiv', qk_, v_)
        R = R * gl_[..., None, None] + jnp.einsum('bhik,bhiv->bhkv', kg_, v_)
        return R, o

    R, o = lax.scan(step, R0, (vc, qg, qk, kg, gl))
    return from_chunks(o, L), R


def rwkv7_scan(r, w, k, v, kk, a, S0):
    def step(S, inp):
        r_, w_, k_, v_, kk_, a_ = inp
        sk = jnp.einsum('bhij,bhj->bhi', S, kk_)
        S = S * w_[:, :, None, :] - sk[..., :, None] * (kk_ * a_)[..., None, :] + v_[..., :, None] * k_[..., None, :]
        return S, jnp.einsum('bhij,bhj->bhi', S, r_)

    xs = tuple(jnp.moveaxis(t, 1, 0) for t in (r, w, k, v, kk, a))
    S, o = lax.scan(step, S0, xs)
    return jnp.moveaxis(o, 0, 1), S


def gdn_mixer(qkv, z, b_raw, a_raw, conv_buf, S0, conv_w, a_log, dt_bias, norm_w):
    B, L, _ = qkv.shape
    cq, new_buf = causal_conv(qkv, conv_buf, conv_w)
    cq = jax.nn.silu(cq)
    q, k, v = (t.reshape(B, L, N_HEADS, HEAD_DIM) for t in jnp.split(cq, 3, axis=-1))
    q = l2norm(q) * HEAD_DIM ** -0.5
    k = l2norm(k)
    beta = jax.nn.sigmoid(b_raw)
    g = -jnp.exp(a_log) * jax.nn.softplus(a_raw + dt_bias)
    o, S = gated_delta_rule(q, k, v, g, beta, S0)
    zh = z.reshape(B, L, N_HEADS, HEAD_DIM)
    o = o * lax.rsqrt(jnp.mean(o * o, axis=-1, keepdims=True) + 1e-6) * norm_w * jax.nn.silu(zh)
    return o.reshape(B, L, GROUP_WIDTH), new_buf, S


def rwkv7_mixer(p_rw, shift_buf, S0, mu, w0, w_up, a0, a_up, g_up, k_k, k_a, r_k, ln_w, ln_b):
    B, L, _ = p_rw.shape
    prev = jnp.concatenate([shift_buf, p_rw[:, :-1]], axis=1)
    xs = p_rw + (prev - p_rw) * mu
    r, k, v, wd, ad, gd = jnp.split(xs, RWKV_SPLITS, axis=-1)
    w = jnp.exp(-RWKV_W_SCALE * jax.nn.sigmoid(w0 + jnp.tanh(wd) @ w_up))
    a = jax.nn.sigmoid(a0 + ad @ a_up)
    gate = jax.nn.sigmoid(gd) @ g_up

    def heads(t):
        return t.reshape(B, L, N_HEADS, HEAD_DIM)

    kk = l2norm(heads(k * k_k))
    k = k * (1.0 + (a - 1.0) * k_a)
    rh, kh, vh = heads(r), heads(k), heads(v)
    o, S = rwkv7_scan(rh, heads(w), kh, vh, kk, heads(a), S0)
    o = groupnorm_heads(o, ln_w, ln_b, RWKV_LN_EPS)
    bonus = jnp.sum(rh * kh * r_k, axis=-1, keepdims=True) * vh
    o = o + bonus.reshape(B, L, GROUP_WIDTH)
    return o * gate, p_rw[:, -1:], S


def rglru_mixer(xb, gb, conv_buf, h0, conv_w, conv_b, wa, ba, wx, bx, lam):
    B, L, _ = xb.shape
    xc, new_buf = causal_conv(xb, conv_buf, conv_w)
    xc = xc + conv_b
    xh = xc.reshape(B, L, LRU_BLOCKS, LRU_BD)
    r = jax.nn.sigmoid(jnp.einsum('blni,nij->blnj', xh, wa).reshape(B, L, GROUP_WIDTH) + ba)
    i = jax.nn.sigmoid(jnp.einsum('blni,nij->blnj', xh, wx).reshape(B, L, GROUP_WIDTH) + bx)
    log_a = -LRU_C * r * jax.nn.softplus(-lam)
    a = jnp.exp(log_a)
    bt = jnp.sqrt(-jnp.expm1(2.0 * log_a)) * (i * xc)
    bt = bt.at[:, 0].add(a[:, 0] * h0)

    def combine(left, right):
        a1, b1 = left
        a2, b2 = right
        return a1 * a2, a2 * b1 + b2

    _, h = lax.associative_scan(combine, (a, bt), axis=1)
    return h * jax.nn.gelu(gb), new_buf, h[:, -1]


def retention_mixer(qkv, gate, R0, pos, norm_w, norm_b):
    B, L, _ = qkv.shape
    q, k, v = (t.reshape(B, L, N_HEADS, HEAD_DIM) for t in jnp.split(qkv, 3, axis=-1))
    q = rotary(q, pos)
    k = rotary(k, pos) * HEAD_DIM ** -0.5
    log_gamma = jnp.log1p(-jnp.exp2(-5.0 - jnp.arange(N_HEADS, dtype=jnp.float32)))
    g = jnp.broadcast_to(log_gamma, (B, L, N_HEADS))
    o, R = retention_chunked(q, k, v, g, R0)
    o = groupnorm_heads(o, norm_w, norm_b, 1e-5)
    return jax.nn.silu(gate) * o, R


def hier_moe(h, rg, rg_b, re, re_b, w_gate, w_up, w_down):
    B, L, D = h.shape
    f32 = jnp.float32
    t = h.reshape(B * L, D)
    lg = (t @ rg).astype(f32) + rg_b
    grp = jnp.argmax(lg, axis=-1)
    p_grp = jnp.max(jax.nn.softmax(lg, axis=-1), axis=-1, keepdims=True)
    le = ((t @ re).astype(f32) + re_b).reshape(-1, N_GROUPS, EXPERTS_PER_GROUP)
    le_g = jnp.einsum('tge,tg->te', le, jax.nn.one_hot(grp, N_GROUPS, dtype=f32))
    top_v, top_i = lax.top_k(le_g, TOP_K_IN_GROUP)
    wts = jax.nn.softmax(top_v, axis=-1) * p_grp
    eidx = grp[:, None] * EXPERTS_PER_GROUP + top_i
    gates = jnp.einsum('tke,tk->te', jax.nn.one_hot(eidx, N_EXPERTS, dtype=f32), wts).astype(h.dtype)
    y = jnp.zeros_like(t)
    for e in range(N_EXPERTS):
        he = jax.nn.silu(t @ w_gate[e]) * (t @ w_up[e])
        y = y + gates[:, e:e + 1] * (he @ w_down[e])
    return y.reshape(B, L, D)


def layer(x, s, pos, p):
    dt = x.dtype
    f32 = jnp.float32
    h = rmsnorm(x, p["norm_mix_w"])
    proj = jnp.einsum('bld,dn->bln', h, p["w_in"]).astype(f32)
    a_qkv, a_z, a_b, a_a, b_rw, c_x, c_g, d_qkv, d_g = jnp.split(proj, np.cumsum(IN_SIZES)[:-1].tolist(), axis=-1)
    o_a, gdn_conv, gdn = gdn_mixer(a_qkv, a_z, a_b, a_a, s["gdn_conv"].astype(f32), s["gdn"].astype(f32),
                                   p["gdn_conv_w"], p["gdn_a_log"], p["gdn_dt_bias"], p["gdn_norm_w"])
    o_b, rwkv_shift, rwkv = rwkv7_mixer(b_rw, s["rwkv_shift"].astype(f32), s["rwkv"].astype(f32),
                                        p["rwkv_mu"], p["rwkv_w0"], p["rwkv_w_up"], p["rwkv_a0"], p["rwkv_a_up"],
                                        p["rwkv_g_up"], p["rwkv_k_k"], p["rwkv_k_a"], p["rwkv_r_k"],
                                        p["rwkv_ln_w"], p["rwkv_ln_b"])
    o_c, lru_conv, lru = rglru_mixer(c_x, c_g, s["lru_conv"].astype(f32), s["lru"].astype(f32),
                                     p["lru_conv_w"], p["lru_conv_b"], p["lru_wa"], p["lru_ba"],
                                     p["lru_wx"], p["lru_bx"], p["lru_lambda"])
    o_d, ret = retention_mixer(d_qkv, d_g, s["ret"].astype(f32), pos, p["ret_norm_w"], p["ret_norm_b"])
    mix = jnp.concatenate([o_a, o_b, o_c, o_d], axis=-1).astype(dt)
    x = x + jnp.einsum('blm,md->bld', mix, p["w_out"])
    x = x + hier_moe(rmsnorm(x, p["norm_ffn_w"]), p["moe_router_g"], p["moe_router_g_b"], p["moe_router_e"],
                     p["moe_router_e_b"], p["moe_w_gate"], p["moe_w_up"], p["moe_w_down"])
    new = {"gdn": gdn.astype(dt), "gdn_conv": gdn_conv.astype(dt), "rwkv": rwkv.astype(dt),
           "rwkv_shift": rwkv_shift.astype(dt), "lru": lru.astype(dt), "lru_conv": lru_conv.astype(dt),
           "ret": ret.astype(dt)}
    return x, new


def trunk(x, states, pos, params, norm_final_w):
    new = {name: [] for name in STATE_NAMES}
    for l in range(DEPTH):
        p = {name: w[l] for name, w in params.items()}
        s = {name: st[l] for name, st in states.items()}
        x, ns = layer(x, s, pos, p)
        for name in STATE_NAMES:
            new[name].append(ns[name])
    return rmsnorm(x, norm_final_w), {name: jnp.stack(v) for name, v in new.items()}


def zero_states(batch, dtype):
    return {"gdn": jnp.zeros((DEPTH, batch, N_HEADS, HEAD_DIM, HEAD_DIM), dtype),
            "gdn_conv": jnp.zeros((DEPTH, batch, CONV_WIDTH - 1, 3 * GROUP_WIDTH), dtype),
            "rwkv": jnp.zeros((DEPTH, batch, N_HEADS, HEAD_DIM, HEAD_DIM), dtype),
            "rwkv_shift": jnp.zeros((DEPTH, batch, 1, RWKV_IN), dtype),
            "lru": jnp.zeros((DEPTH, batch, GROUP_WIDTH), dtype),
            "lru_conv": jnp.zeros((DEPTH, batch, CONV_WIDTH - 1, GROUP_WIDTH), dtype),
            "ret": jnp.zeros((DEPTH, batch, N_HEADS, HEAD_DIM, HEAD_DIM), dtype)}


def setup_inputs(seed: int = 0) -> dict:
    key = jax.random.key(seed)
    ks = iter(jax.random.split(key, 64))
    f32 = jnp.float32
    GW = GROUP_WIDTH

    def nrm(shape, scale):
        return scale * jax.random.normal(next(ks), shape, f32)

    def unif(shape, lo, hi):
        return jax.random.uniform(next(ks), shape, f32, lo, hi)

    inp = {}
    inp["x_prompt"] = nrm((BATCH, SEQ, D_MODEL), 1.0)
    inp["x_sample"] = nrm((DEC_BATCH, DEC_SEQ, D_MODEL), 1.0)
    inp["state_gdn"] = nrm((DEPTH, DEC_BATCH, N_HEADS, HEAD_DIM, HEAD_DIM), 0.1)
    inp["state_gdn_conv"] = nrm((DEPTH, DEC_BATCH, CONV_WIDTH - 1, 3 * GW), 1.0)
    inp["state_rwkv"] = nrm((DEPTH, DEC_BATCH, N_HEADS, HEAD_DIM, HEAD_DIM), 0.1)
    inp["state_rwkv_shift"] = nrm((DEPTH, DEC_BATCH, 1, RWKV_IN), 1.0)
    inp["state_lru"] = nrm((DEPTH, DEC_BATCH, GW), 0.5)
    inp["state_lru_conv"] = nrm((DEPTH, DEC_BATCH, CONV_WIDTH - 1, GW), 1.0)
    inp["state_ret"] = nrm((DEPTH, DEC_BATCH, N_HEADS, HEAD_DIM, HEAD_DIM), 0.5)
    inp["norm_mix_w"] = 1.0 + nrm((DEPTH, D_MODEL), 0.02)
    inp["w_in"] = nrm((DEPTH, D_MODEL, N_IN), D_MODEL ** -0.5)
    inp["gdn_conv_w"] = nrm((DEPTH, CONV_WIDTH, 3 * GW), CONV_WIDTH ** -0.5)
    inp["gdn_a_log"] = jnp.log(unif((DEPTH, N_HEADS), 1.0, 16.0))
    dt0 = jnp.exp(unif((DEPTH, N_HEADS), math.log(1e-3), math.log(1e-1)))
    inp["gdn_dt_bias"] = jnp.log(jnp.expm1(dt0))
    inp["gdn_norm_w"] = 1.0 + nrm((DEPTH, HEAD_DIM), 0.02)
    inp["rwkv_mu"] = unif((DEPTH, RWKV_IN), 0.0, 1.0)
    inp["rwkv_w0"] = nrm((DEPTH, GW), 0.5)
    inp["rwkv_w_up"] = nrm((DEPTH, RWKV_DECAY_LORA, GW), 0.5 * RWKV_DECAY_LORA ** -0.5)
    inp["rwkv_a0"] = nrm((DEPTH, GW), 0.5)
    inp["rwkv_a_up"] = nrm((DEPTH, RWKV_AAA_LORA, GW), 0.5 * RWKV_AAA_LORA ** -0.5)
    inp["rwkv_g_up"] = nrm((DEPTH, RWKV_GATE_LORA, GW), RWKV_GATE_LORA ** -0.5)
    inp["rwkv_k_k"] = 0.85 + nrm((DEPTH, GW), 0.05)
    inp["rwkv_k_a"] = 1.0 + nrm((DEPTH, GW), 0.05)
    inp["rwkv_r_k"] = nrm((DEPTH, N_HEADS, HEAD_DIM), 0.1)
    inp["rwkv_ln_w"] = 1.0 + nrm((DEPTH, GW), 0.02)
    inp["rwkv_ln_b"] = nrm((DEPTH, GW), 0.01)
    inp["lru_conv_w"] = nrm((DEPTH, CONV_WIDTH, GW), CONV_WIDTH ** -0.5)
    inp["lru_conv_b"] = nrm((DEPTH, GW), 0.01)
    inp["lru_wa"] = nrm((DEPTH, LRU_BLOCKS, LRU_BD, LRU_BD), LRU_BD ** -0.5)
    inp["lru_ba"] = nrm((DEPTH, GW), 0.01)
    inp["lru_wx"] = nrm((DEPTH, LRU_BLOCKS, LRU_BD, LRU_BD), LRU_BD ** -0.5)
    inp["lru_bx"] = nrm((DEPTH, GW), 0.01)
    root = unif((DEPTH, GW), 0.9, 0.999) ** (1.0 / LRU_C)
    inp["lru_lambda"] = jnp.log(root) - jnp.log1p(-root)
    inp["ret_norm_w"] = 1.0 + nrm((DEPTH, GW), 0.02)
    inp["ret_norm_b"] = nrm((DEPTH, GW), 0.01)
    inp["w_out"] = nrm((DEPTH, D_MIX, D_MODEL), D_MIX ** -0.5)
    inp["norm_ffn_w"] = 1.0 + nrm((DEPTH, D_MODEL), 0.02)
    inp["moe_router_g"] = nrm((DEPTH, D_MODEL, N_GROUPS), D_MODEL ** -0.5)
    inp["moe_router_g_b"] = nrm((DEPTH, N_GROUPS), 0.01)
    inp["moe_router_e"] = nrm((DEPTH, D_MODEL, N_EXPERTS), D_MODEL ** -0.5)
    inp["moe_router_e_b"] = nrm((DEPTH, N_EXPERTS), 0.01)
    inp["moe_w_gate"] = nrm((DEPTH, N_EXPERTS, D_MODEL, D_EXPERT), D_MODEL ** -0.5)
    inp["moe_w_up"] = nrm((DEPTH, N_EXPERTS, D_MODEL, D_EXPERT), D_MODEL ** -0.5)
    inp["moe_w_down"] = nrm((DEPTH, N_EXPERTS, D_EXPERT, D_MODEL), D_EXPERT ** -0.5)
    inp["norm_final_w"] = 1.0 + nrm((D_MODEL,), 0.02)
    return inp


def reference(x_prompt, x_sample, state_gdn, state_gdn_conv, state_rwkv, state_rwkv_shift, state_lru,
              state_lru_conv, state_ret, norm_mix_w, w_in, gdn_conv_w, gdn_a_log, gdn_dt_bias, gdn_norm_w,
              rwkv_mu, rwkv_w0, rwkv_w_up, rwkv_a0, rwkv_a_up, rwkv_g_up, rwkv_k_k, rwkv_k_a, rwkv_r_k,
              rwkv_ln_w, rwkv_ln_b, lru_conv_w, lru_conv_b, lru_wa, lru_ba, lru_wx, lru_bx, lru_lambda,
              ret_norm_w, ret_norm_b, w_out, norm_ffn_w, moe_router_g, moe_router_g_b, moe_router_e,
              moe_router_e_b, moe_w_gate, moe_w_up, moe_w_down, norm_final_w):
    params = {"norm_mix_w": norm_mix_w, "w_in": w_in, "gdn_conv_w": gdn_conv_w, "gdn_a_log": gdn_a_log,
              "gdn_dt_bias": gdn_dt_bias, "gdn_norm_w": gdn_norm_w, "rwkv_mu": rwkv_mu, "rwkv_w0": rwkv_w0,
              "rwkv_w_up": rwkv_w_up, "rwkv_a0": rwkv_a0, "rwkv_a_up": rwkv_a_up, "rwkv_g_up": rwkv_g_up,
              "rwkv_k_k": rwkv_k_k, "rwkv_k_a": rwkv_k_a, "rwkv_r_k": rwkv_r_k, "rwkv_ln_w": rwkv_ln_w,
              "rwkv_ln_b": rwkv_ln_b, "lru_conv_w": lru_conv_w, "lru_conv_b": lru_conv_b, "lru_wa": lru_wa,
              "lru_ba": lru_ba, "lru_wx": lru_wx, "lru_bx": lru_bx, "lru_lambda": lru_lambda,
              "ret_norm_w": ret_norm_w, "ret_norm_b": ret_norm_b, "w_out": w_out, "norm_ffn_w": norm_ffn_w,
              "moe_router_g": moe_router_g, "moe_router_g_b": moe_router_g_b, "moe_router_e": moe_router_e,
              "moe_router_e_b": moe_router_e_b, "moe_w_gate": moe_w_gate, "moe_w_up": moe_w_up,
              "moe_w_down": moe_w_down}
    pos_p = jnp.arange(x_prompt.shape[1], dtype=jnp.int32)
    y_prompt, new_p = trunk(x_prompt, zero_states(x_prompt.shape[0], x_prompt.dtype), pos_p, params, norm_final_w)
    states_s = {"gdn": state_gdn, "gdn_conv": state_gdn_conv, "rwkv": state_rwkv, "rwkv_shift": state_rwkv_shift,
                "lru": state_lru, "lru_conv": state_lru_conv, "ret": state_ret}
    pos_s = PAST_LEN + jnp.arange(x_sample.shape[1], dtype=jnp.int32)
    y_sample, new_s = trunk(x_sample, states_s, pos_s, params, norm_final_w)
    return (y_prompt, y_sample,
            new_p["gdn"], new_s["gdn"], new_p["gdn_conv"], new_s["gdn_conv"],
            new_p["rwkv"], new_s["rwkv"], new_p["rwkv_shift"], new_s["rwkv_shift"],
            new_p["lru"], new_s["lru"], new_p["lru_conv"], new_s["lru_conv"],
            new_p["ret"], new_s["ret"])
```

```python
import functools

import jax
import jax.numpy as jnp
from jax import lax
from jax.experimental import pallas as pl
from jax.experimental.pallas import tpu as pltpu

f32 = jnp.float32
bf16 = jnp.bfloat16
HIGHEST = lax.Precision.HIGHEST

D_MODEL = 1024
N_HEADS = 4
HEAD_DIM = 64
GW = N_HEADS * HEAD_DIM
CHUNK = 64
DEPTH = 2
PAST_LEN = 4096
RWKV_W_SCALE = 0.606531
RWKV_LN_EPS = 64e-5
LRU_C = 8.0
ROPE_BASE = 10000.0
N_GROUPS = 4
EXPERTS_PER_GROUP = 4
N_EXPERTS = 16
D_EXPERT = 512

PA_W = 3 * GW + GW + 128
PB_W = 1024
PC_W = 2 * GW
PD_W = 3 * GW + GW
P_ALL = PA_W + PB_W + PC_W + PD_W

VMEM_LIMIT_BYTES = 56 * 1024 * 1024
SEQ_PER_STEP = 4

NN = (((1,), (0,)), ((), ()))
NT = (((1,), (1,)), ((), ()))
TN = (((0,), (0,)), ((), ()))


def _dot(a, b, dims=NN, precision=None):
    return lax.dot_general(a, b, dims, precision=precision, preferred_element_type=f32)


def _p1(x):
    return (x.astype(bf16),)


def _p3(x):
    hi = x.astype(bf16)
    return (hi, (x - hi.astype(f32)).astype(bf16))


def _mm(ap, bp, dims=NN):
    out = _dot(ap[0], bp[0], dims)
    if len(bp) > 1:
        out = out + _dot(ap[0], bp[1], dims)
    if len(ap) > 1:
        out = out + _dot(ap[1], bp[0], dims)
    return out


def _iota(shape, dim):
    return lax.broadcasted_iota(jnp.int32, shape, dim)


def _masks():
    t = _iota((CHUNK, GW), 0)
    lane = _iota((CHUNK, GW), 1)
    s = lane & (HEAD_DIM - 1)
    r2 = _iota((GW, GW), 0) >> 6
    c2 = _iota((GW, GW), 1) >> 6
    bd = r2 == c2
    ts = _iota((CHUNK, CHUNK), 0)
    ss = _iota((CHUNK, CHUNK), 1)
    return dict(
        t=t, lane=lane, s=s, bd=bd,
        incl=t >= s, strict=t > s,
        blk16=(t >> 4) == (s >> 4), blk32=(t >> 5) == (s >> 5),
        bones=jnp.where(bd, 1.0, 0.0).astype(bf16),
        lt=jnp.where(ts >= ss, 1.0, 0.0).astype(f32),
        ones64=jnp.ones((CHUNK, CHUNK), f32),
        itile=jnp.where(t == s, 1.0, 0.0).astype(f32),
    )


def _bd(parts, m):
    return tuple(
        jnp.where(m["bd"], jnp.concatenate([p.astype(f32)] * N_HEADS, axis=0), 0.0).astype(bf16)
        for p in parts)


def _segsum(x, m):
    return _mm(_p3(x), (m["bones"],))


def _cumsum_t(x, m):
    return _dot(m["lt"], x, precision=HIGHEST)


def _softplus(x):
    return jnp.maximum(x, 0.0) + jnp.log1p(jnp.exp(-jnp.abs(x)))


def _tri_inv_q(a, m, prep):
    m1 = -jnp.where(m["blk16"], a, 0.0)
    b1 = _bd(prep(m1), m)
    m2 = _mm(prep(m1), b1)
    b2 = _bd(prep(m2), m)
    m4 = _mm(prep(m2), b2)
    b4 = _bd(prep(m4), m)
    m8 = _mm(prep(m4), b4)
    b8 = _bd(prep(m8), m)
    q = m1
    q = q + m2 + _mm(prep(q), b2)
    q = q + m4 + _mm(prep(q), b4)
    q = q + m8 + _mm(prep(q), b8)
    for e in (jnp.where(m["blk32"] & jnp.logical_not(m["blk16"]), a, 0.0),
              jnp.where(m["blk32"], 0.0, a)):
        x = e + _mm(prep(q), _bd(prep(e), m))
        y = x + _mm(prep(x), _bd(prep(q), m))
        q = q - y
    return q


PREP_INV = _p3
PREP = _p1


def _cparams(sem):
    return pltpu.CompilerParams(dimension_semantics=sem, vmem_limit_bytes=VMEM_LIMIT_BYTES)


def _proj_kernel(x_ref, nw_ref, w_ref, pa_ref, pb_ref, pc_ref, pd_ref):
    x = x_ref[...]
    h = (x * lax.rsqrt(jnp.mean(x * x, axis=-1, keepdims=True) + 1e-6) * nw_ref[...]).astype(bf16)
    off = 0
    for ref, width in ((pa_ref, PA_W), (pb_ref, PB_W), (pc_ref, PC_W), (pd_ref, PD_W)):
        ref[...] = _dot(h, w_ref[:, off:off + width])
        off += width


def _proj(x2d, nw, w_all, tm):
    t = x2d.shape[0]
    widths = (PA_W, PB_W, PC_W, PD_W)
    return pl.pallas_call(
        _proj_kernel,
        grid=(t // tm,),
        in_specs=[pl.BlockSpec((tm, D_MODEL), lambda i: (i, 0)),
                  pl.BlockSpec((1, D_MODEL), lambda i: (0, 0)),
                  pl.BlockSpec((D_MODEL, P_ALL), lambda i: (0, 0))],
        out_specs=[pl.BlockSpec((tm, w), lambda i: (i, 0)) for w in widths],
        out_shape=[jax.ShapeDtypeStruct((t, w), f32) for w in widths],
        compiler_params=_cparams(("parallel",)),
        name="proj",
    )(x2d, nw, w_all)


def _conv4(xb_ref, b, rows, cw):
    y = xb_ref[b, 5:5 + rows, :] * cw[0:1, :]
    for j in range(1, 4):
        y = y + xb_ref[b, 5 + j:5 + j + rows, :] * cw[j:j + 1, :]
    return y


def _gdn_kernel(pa_ref, cb_ref, s0_ref, cw_ref, vec_ref, o_ref, sout_ref, xb_ref, s_ref, *, nb):
    c = pl.program_id(1)
    m = _masks()

    @pl.when(c == 0)
    def _():
        s_ref[...] = s0_ref[...]
        xb_ref[:, 0:8, :] = cb_ref[...]

    cw = cw_ref[...]
    vec = vec_ref[...]
    neg_exp_alog, dt_bias, norm_w = vec[0:1, :], vec[1:2, :], vec[2:3, :]
    er = _iota((128, GW), 0)
    eh = _iota((128, GW), 1) >> 6
    exp_b = jnp.where(er == eh, 1.0, 0.0).astype(f32)
    exp_a = jnp.where(er == eh + N_HEADS, 1.0, 0.0).astype(f32)

    for b in range(nb):
        xb_ref[b, 8:8 + CHUNK, :] = pa_ref[b, :, 0:3 * GW]
        y = _conv4(xb_ref, b, CHUNK, cw)
        xb_ref[b, 0:8, :] = xb_ref[b, CHUNK:CHUNK + 8, :]
        cq = y * jax.nn.sigmoid(y)
        q, k, v = cq[:, 0:GW], cq[:, GW:2 * GW], cq[:, 2 * GW:3 * GW]
        z = pa_ref[b, :, 3 * GW:4 * GW]
        misc = pa_ref[b, :, 4 * GW:4 * GW + 128]
        q = q * lax.rsqrt(_segsum(q * q, m) + 1e-6) * (HEAD_DIM ** -0.5)
        k = k * lax.rsqrt(_segsum(k * k, m) + 1e-6)
        beta = jax.nn.sigmoid(_dot(misc, exp_b, precision=HIGHEST))
        g = neg_exp_alog * _softplus(_dot(misc, exp_a, precision=HIGHEST) + dt_bias)
        gc = _cumsum_t(g, m)
        grow = _dot(m["ones64"], gc * m["itile"], precision=HIGHEST)
        decay = jnp.exp(jnp.where(m["incl"], gc - grow, -jnp.inf))
        eg = jnp.exp(gc)
        glast = gc[CHUNK - 1:CHUNK, :]
        kb = k * beta
        bk = _bd(PREP(k), m)
        a = jnp.where(m["strict"], _mm(PREP(kb), bk, NT) * decay, 0.0)
        qt = _tri_inv_q(a, m, PREP_INV)
        rhs_w = kb * eg
        rhs_u = v * beta
        w = rhs_w + _mm(PREP_INV(qt), _bd(PREP_INV(rhs_w), m))
        u = rhs_u + _mm(PREP_INV(qt), _bd(PREP_INV(rhs_u), m))
        qk = _mm(PREP(q), bk, NT) * decay
        qg = q * eg
        kg = k * jnp.exp(glast - gc)
        s_bd = s_ref[b]
        sp = PREP(s_bd)
        vn = u - _mm(PREP(w), sp)
        o = _mm(PREP(qg), sp) + _mm(PREP(qk), _bd(PREP(vn), m))
        s_new = s_bd * jnp.exp(glast) + jnp.where(m["bd"], _mm(PREP(kg), PREP(vn), TN), 0.0)
        s_ref[b] = s_new
        o = o * lax.rsqrt(_segsum(o * o, m) * (1.0 / HEAD_DIM) + 1e-6) * norm_w * (z * jax.nn.sigmoid(z))
        o_ref[b] = o

    @pl.when(c == pl.num_programs(1) - 1)
    def _():
        sout_ref[...] = s_ref[...]


def _gdn(pa, cb8, s0_bd, cw, vec):
    bsz, length, _ = pa.shape
    nb = SEQ_PER_STEP
    grid = (bsz // nb, length // CHUNK)
    return pl.pallas_call(
        functools.partial(_gdn_kernel, nb=nb),
        grid=grid,
        in_specs=[pl.BlockSpec((nb, CHUNK, PA_W), lambda i, c: (i, c, 0)),
                  pl.BlockSpec((nb, 8, 3 * GW), lambda i, c: (i, 0, 0)),
                  pl.BlockSpec((nb, GW, GW), lambda i, c: (i, 0, 0)),
                  pl.BlockSpec((4, 3 * GW), lambda i, c: (0, 0)),
                  pl.BlockSpec((8, GW), lambda i, c: (0, 0))],
        out_specs=[pl.BlockSpec((nb, CHUNK, GW), lambda i, c: (i, c, 0)),
                   pl.BlockSpec((nb, GW, GW), lambda i, c: (i, 0, 0))],
        out_shape=[jax.ShapeDtypeStruct((bsz, length, GW), f32),
                   jax.ShapeDtypeStruct((bsz, GW, GW), f32)],
        scratch_shapes=[pltpu.VMEM((nb, CHUNK + 8, 3 * GW), f32),
                        pltpu.VMEM((nb, GW, GW), f32)],
        compiler_params=_cparams(("arbitrary", "arbitrary")),
        name="gdn",
    )(pa, cb8, s0_bd, cw, vec)


def _rwkv_kernel(pb_ref, sh_ref, s0_ref, vec_ref, mu_ref, wup_ref, aup_ref, gup_ref,
                 o_ref, sout_ref, prev_ref, s_ref, *, nb):
    c = pl.program_id(1)
    m = _masks()

    @pl.when(c == 0)
    def _():
        s_ref[...] = s0_ref[...]
        prev_ref[...] = sh_ref[...]

    vec = vec_ref[...]
    w0, a0, k_k, k_a, r_k, ln_w, ln_b = (vec[i:i + 1, :] for i in range(7))
    mu = mu_ref[...]
    t1024 = _iota((CHUNK, PB_W), 0)

    for b in range(nb):
        p = pb_ref[b]
        prev = jnp.where(t1024 == 0, prev_ref[b], pltpu.roll(p, 1, 0))
        prev_ref[b] = p[CHUNK - 1:CHUNK, :]
        xs = p + (prev - p) * mu
        r, k, v = xs[:, 0:GW], xs[:, GW:2 * GW], xs[:, 2 * GW:3 * GW]
        wd, ad, gd = xs[:, 768:832], xs[:, 832:896], xs[:, 896:1024]
        logw = -RWKV_W_SCALE * jax.nn.sigmoid(w0 + _mm(_p3(jnp.tanh(wd)), _p3(wup_ref[...])))
        a = jax.nn.sigmoid(a0 + _mm(_p3(ad), _p3(aup_ref[...])))
        gate = _mm(_p3(jax.nn.sigmoid(gd)), _p3(gup_ref[...]))
        kk = k * k_k
        kk = kk * lax.rsqrt(_segsum(kk * kk, m) + 1e-6)
        k = k * (1.0 + (a - 1.0) * k_a)

        gc = _cumsum_t(logw, m)
        glast = gc[CHUNK - 1:CHUNK, :]
        eng = jnp.exp(-gc)
        kkd = kk * jnp.exp(gc - logw)
        bvec = a * kk
        binv = bvec * eng
        kinv = k * eng
        rd = r * jnp.exp(gc)
        lhs2 = PREP(jnp.concatenate([kkd, rd], axis=0))
        xb_ = _mm(lhs2, _bd(PREP(binv), m), NT)
        xk_ = _mm(lhs2, _bd(PREP(kinv), m), NT)
        amat = jnp.where(m["strict"], xb_[0:CHUNK], 0.0)
        bmat = jnp.where(m["strict"], xk_[0:CHUNK], 0.0)
        rb = jnp.where(m["incl"], xb_[CHUNK:], 0.0)
        rk = jnp.where(m["incl"], xk_[CHUNK:], 0.0)
        qt = _tri_inv_q(amat, m, PREP_INV)

        s_bd = s_ref[b]
        sp = PREP(s_bd)
        bv = _bd(PREP(v), m)
        rhs = _mm(PREP(kkd), sp, NT) + _mm(PREP(bmat), bv)
        u = -(rhs + _mm(PREP_INV(qt), _bd(PREP_INV(rhs), m)))
        o = _mm(PREP(rd), sp, NT) + _mm(PREP(rb), _bd(PREP(u), m)) + _mm(PREP(rk), bv)
        dec_end = jnp.exp(glast - gc)
        upd = _mm(PREP(jnp.concatenate([u, v], axis=0)),
                  PREP(jnp.concatenate([bvec * dec_end, k * dec_end], axis=0)), TN)
        s_ref[b] = s_bd * jnp.exp(glast) + jnp.where(m["bd"], upd, 0.0)

        mean = _segsum(o, m) * (1.0 / HEAD_DIM)
        oc = o - mean
        var = _segsum(oc * oc, m) * (1.0 / HEAD_DIM)
        on = oc * lax.rsqrt(var + RWKV_LN_EPS) * ln_w + ln_b
        bonus = _segsum(r * k * r_k, m) * v
        o_ref[b] = (on + bonus) * gate

    @pl.when(c == pl.num_programs(1) - 1)
    def _():
        sout_ref[...] = s_ref[...]


def _rwkv(pb, shift, s0_bd, vec, mu, wup, aup, gup):
    bsz, length, _ = pb.shape
    nb = SEQ_PER_STEP
    grid = (bsz // nb, length // CHUNK)
    full = lambda shape: pl.BlockSpec(shape, lambda i, c: tuple(0 for _ in shape))
    return pl.pallas_call(
        functools.partial(_rwkv_kernel, nb=nb),
        grid=grid,
        in_specs=[pl.BlockSpec((nb, CHUNK, PB_W), lambda i, c: (i, c, 0)),
                  pl.BlockSpec((nb, 1, PB_W), lambda i, c: (i, 0, 0)),
                  pl.BlockSpec((nb, GW, GW), lambda i, c: (i, 0, 0)),
                  full((8, GW)), full((1, PB_W)), full((64, GW)), full((64, GW)), full((128, GW))],
        out_specs=[pl.BlockSpec((nb, CHUNK, GW), lambda i, c: (i, c, 0)),
                   pl.BlockSpec((nb, GW, GW), lambda i, c: (i, 0, 0))],
        out_shape=[jax.ShapeDtypeStruct((bsz, length, GW), f32),
                   jax.ShapeDtypeStruct((bsz, GW, GW), f32)],
        scratch_shapes=[pltpu.VMEM((nb, 1, PB_W), f32),
                        pltpu.VMEM((nb, GW, GW), f32)],
        compiler_params=_cparams(("arbitrary", "arbitrary")),
        name="rwkv",
    )(pb, shift, s0_bd, vec, mu, wup, aup, gup)


def _neg_expm1(x):
    u = jnp.exp(x)
    um1 = u - 1.0
    lu = jnp.log(u)
    safe = jnp.where(um1 == 0.0, x, um1 * x / jnp.where(lu == 0.0, 1.0, lu))
    return -jnp.where(x < -0.5, um1, safe)


def _lru_kernel(pc_ref, cb_ref, h0_ref, cw_ref, vec_ref, wa_ref, wx_ref,
                o_ref, hout_ref, xb_ref, h_ref, *, nb, rows):
    c = pl.program_id(1)

    @pl.when(c == 0)
    def _():
        h_ref[...] = h0_ref[...]
        xb_ref[:, 0:8, :] = cb_ref[...]

    cw = cw_ref[...]
    vec = vec_ref[...]
    conv_b, ba, bx, sp_neg_lam = (vec[i:i + 1, :] for i in range(4))
    t = _iota((rows, GW), 0)

    for b in range(nb):
        xb_ref[b, 8:8 + rows, :] = pc_ref[b, :, 0:GW]
        xc = _conv4(xb_ref, b, rows, cw) + conv_b
        xb_ref[b, 0:8, :] = xb_ref[b, rows:rows + 8, :]
        gb = pc_ref[b, :, GW:2 * GW]
        xcp = PREP(xc)
        r = jax.nn.sigmoid(_mm(xcp, (wa_ref[...],)) + ba)
        i = jax.nn.sigmoid(_mm(xcp, (wx_ref[...],)) + bx)
        log_a = -LRU_C * r * sp_neg_lam
        a = jnp.exp(log_a)
        bt = jnp.sqrt(_neg_expm1(2.0 * log_a)) * (i * xc)
        d = 1
        while d < rows:
            keep = t >= d
            a_sh = jnp.where(keep, pltpu.roll(a, d, 0), 1.0)
            b_sh = jnp.where(keep, pltpu.roll(bt, d, 0), 0.0)
            bt = a * b_sh + bt
            a = a * a_sh
            d *= 2
        h = bt + a * h_ref[b]
        h_ref[b] = h[rows - 1:rows, :]
        o_ref[b] = h * jax.nn.gelu(gb)

    @pl.when(c == pl.num_programs(1) - 1)
    def _():
        hout_ref[...] = h_ref[...]


def _lru(pc, cb8, h0, cw, vec, wa_bd, wx_bd):
    bsz, length, _ = pc.shape
    nb = SEQ_PER_STEP
    rows = min(length, 256)
    grid = (bsz // nb, length // rows)
    full = lambda shape: pl.BlockSpec(shape, lambda i, c: tuple(0 for _ in shape))
    return pl.pallas_call(
        functools.partial(_lru_kernel, nb=nb, rows=rows),
        grid=grid,
        in_specs=[pl.BlockSpec((nb, rows, PC_W), lambda i, c: (i, c, 0)),
                  pl.BlockSpec((nb, 8, GW), lambda i, c: (i, 0, 0)),
                  pl.BlockSpec((nb, 1, GW), lambda i, c: (i, 0, 0)),
                  full((4, GW)), full((8, GW)), full((GW, GW)), full((GW, GW))],
        out_specs=[pl.BlockSpec((nb, rows, GW), lambda i, c: (i, c, 0)),
                   pl.BlockSpec((nb, 1, GW), lambda i, c: (i, 0, 0))],
        out_shape=[jax.ShapeDtypeStruct((bsz, length, GW), f32),
                   jax.ShapeDtypeStruct((bsz, 1, GW), f32)],
        scratch_shapes=[pltpu.VMEM((nb, rows + 8, GW), f32),
                        pltpu.VMEM((nb, 1, GW), f32)],
        compiler_params=_cparams(("arbitrary", "arbitrary")),
        name="lru",
    )(pc, cb8, h0, cw, vec, wa_bd, wx_bd)


def _ret_kernel(pd_ref, r0_ref, cos_ref, sin_ref, tab_ref, vec_ref, o_ref, rout_ref, s_ref, *, nb):
    c = pl.program_id(1)
    m = _masks()

    @pl.when(c == 0)
    def _():
        s_ref[...] = r0_ref[...]

    cos = cos_ref[...]
    sin = sin_ref[...]
    dec, qgs, kgs = tab_ref[0], tab_ref[1], tab_ref[2]
    vec = vec_ref[...]
    gl, norm_w, norm_b = vec[0:1, :], vec[1:2, :], vec[2:3, :]
    low_half = m["s"] < (HEAD_DIM // 2)

    def rot(x):
        swapped = jnp.where(low_half, pltpu.roll(x, GW - HEAD_DIM // 2, 1), pltpu.roll(x, HEAD_DIM // 2, 1))
        return x * cos + swapped * sin

    for b in range(nb):
        q = rot(pd_ref[b, :, 0:GW])
        k = rot(pd_ref[b, :, GW:2 * GW]) * (HEAD_DIM ** -0.5)
        v = pd_ref[b, :, 2 * GW:3 * GW]
        gate = pd_ref[b, :, 3 * GW:4 * GW]
        qk = _mm(PREP(q), _bd(PREP(k), m), NT) * dec
        s_bd = s_ref[b]
        o = _mm(PREP(q * qgs), PREP(s_bd)) + _mm(PREP(qk), _bd(PREP(v), m))
        s_ref[b] = s_bd * gl + jnp.where(m["bd"], _mm(PREP(k * kgs), PREP(v), TN), 0.0)
        mean = _segsum(o, m) * (1.0 / HEAD_DIM)
        oc = o - mean
        var = _segsum(oc * oc, m) * (1.0 / HEAD_DIM)
        on = oc * lax.rsqrt(var + 1e-5) * norm_w + norm_b
        o_ref[b] = gate * jax.nn.sigmoid(gate) * on

    @pl.when(c == pl.num_programs(1) - 1)
    def _():
        rout_ref[...] = s_ref[...]


def _ret(pd, r0_bd, cos, sin, tab, vec):
    bsz, length, _ = pd.shape
    nb = SEQ_PER_STEP
    grid = (bsz // nb, length // CHUNK)
    full = lambda shape: pl.BlockSpec(shape, lambda i, c: tuple(0 for _ in shape))
    return pl.pallas_call(
        functools.partial(_ret_kernel, nb=nb),
        grid=grid,
        in_specs=[pl.BlockSpec((nb, CHUNK, PD_W), lambda i, c: (i, c, 0)),
                  pl.BlockSpec((nb, GW, GW), lambda i, c: (i, 0, 0)),
                  pl.BlockSpec((CHUNK, GW), lambda i, c: (c, 0)),
                  pl.BlockSpec((CHUNK, GW), lambda i, c: (c, 0)),
                  full((3, CHUNK, GW)), full((8, GW))],
        out_specs=[pl.BlockSpec((nb, CHUNK, GW), lambda i, c: (i, c, 0)),
                   pl.BlockSpec((nb, GW, GW), lambda i, c: (i, 0, 0))],
        out_shape=[jax.ShapeDtypeStruct((bsz, length, GW), f32),
                   jax.ShapeDtypeStruct((bsz, GW, GW), f32)],
        scratch_shapes=[pltpu.VMEM((nb, GW, GW), f32)],
        compiler_params=_cparams(("arbitrary", "arbitrary")),
        name="ret",
    )(pd, r0_bd, cos, sin, tab, vec)


def _post_kernel(x_ref, oa_ref, ob_ref, oc_ref, od_ref, wout_ref, nw_ref, wr_ref, br_ref,
                 x1_ref, t_ref, gates_ref):
    acc = x_ref[...]
    for j, ref in enumerate((oa_ref, ob_ref, oc_ref, od_ref)):
        acc = acc + _dot(ref[...].astype(bf16), wout_ref[j * GW:(j + 1) * GW, :])
    x1_ref[...] = acc
    tn = acc * lax.rsqrt(jnp.mean(acc * acc, axis=-1, keepdims=True) + 1e-6) * nw_ref[...]
    t_ref[...] = tn.astype(bf16)
    logits = _dot(tn, wr_ref[...], precision=HIGHEST) + br_ref[...]
    lane = _iota(logits.shape, 1)
    ninf = -jnp.inf
    is_g = (lane >= N_EXPERTS) & (lane < N_EXPERTS + N_GROUPS)
    lg = jnp.where(is_g, logits, ninf)
    gmax = jnp.max(lg, axis=-1, keepdims=True)
    grp = jnp.min(jnp.where(lg == gmax, lane, 1 << 20), axis=-1, keepdims=True) - N_EXPERTS
    p_grp = 1.0 / jnp.sum(jnp.exp(lg - gmax), axis=-1, keepdims=True)
    in_grp = (lane >> 2) == grp
    le = jnp.where(in_grp, logits, ninf)
    v1 = jnp.max(le, axis=-1, keepdims=True)
    i1 = jnp.min(jnp.where(le == v1, lane, 1 << 20), axis=-1, keepdims=True)
    le2 = jnp.where(lane == i1, ninf, le)
    v2 = jnp.max(le2, axis=-1, keepdims=True)
    i2 = jnp.min(jnp.where(le2 == v2, lane, 1 << 20), axis=-1, keepdims=True)
    e2 = jnp.exp(v2 - v1)
    wt1 = p_grp / (1.0 + e2)
    wt2 = p_grp * e2 / (1.0 + e2)
    gates_ref[...] = jnp.where(lane == i1, wt1, 0.0) + jnp.where(lane == i2, wt2, 0.0)


def _post(x2d, outs, wout, nw, wr, br, tm):
    t = x2d.shape[0]
    row = lambda w: pl.BlockSpec((tm, w), lambda i: (i, 0))
    full = lambda shape: pl.BlockSpec(shape, lambda i: tuple(0 for _ in shape))
    return pl.pallas_call(
        _post_kernel,
        grid=(t // tm,),
        in_specs=[row(D_MODEL), row(GW), row(GW), row(GW), row(GW),
                  full((D_MODEL, D_MODEL)), full((1, D_MODEL)), full((D_MODEL, 128)), full((1, 128))],
        out_specs=[row(D_MODEL), row(D_MODEL), row(128)],
        out_shape=[jax.ShapeDtypeStruct((t, D_MODEL), f32),
                   jax.ShapeDtypeStruct((t, D_MODEL), bf16),
                   jax.ShapeDtypeStruct((t, 128), f32)],
        compiler_params=_cparams(("parallel",)),
        name="post",
    )(x2d, *outs, wout, nw, wr, br)


def _moe_kernel(t_ref, g_ref, x1_ref, wg_ref, wu_ref, wd_ref, nf_ref, o_ref, *, final_norm):
    e = pl.program_id(1)

    @pl.when(e == 0)
    def _():
        o_ref[...] = x1_ref[...]

    tb = t_ref[...]
    hg = _dot(tb, wg_ref[0])
    hu = _dot(tb, wu_ref[0])
    he = (hg * jax.nn.sigmoid(hg) * hu).astype(bf16)
    y = _dot(he, wd_ref[0])
    gates = g_ref[...]
    lane = _iota(gates.shape, 1)
    ge = jnp.sum(jnp.where(lane == e, gates, 0.0), axis=-1, keepdims=True)
    o_ref[...] += ge * y

    if final_norm:
        @pl.when(e == pl.num_programs(1) - 1)
        def _():
            x = o_ref[...]
            o_ref[...] = x * lax.rsqrt(jnp.mean(x * x, axis=-1, keepdims=True) + 1e-6) * nf_ref[...]


def _moe(tb, gates, x1, wg, wu, wd, nf, tm, final_norm):
    t = tb.shape[0]
    return pl.pallas_call(
        functools.partial(_moe_kernel, final_norm=final_norm),
        grid=(t // tm, N_EXPERTS),
        in_specs=[pl.BlockSpec((tm, D_MODEL), lambda i, e: (i, 0)),
                  pl.BlockSpec((tm, 128), lambda i, e: (i, 0)),
                  pl.BlockSpec((tm, D_MODEL), lambda i, e: (i, 0)),
                  pl.BlockSpec((1, D_MODEL, D_EXPERT), lambda i, e: (e, 0, 0)),
                  pl.BlockSpec((1, D_MODEL, D_EXPERT), lambda i, e: (e, 0, 0)),
                  pl.BlockSpec((1, D_EXPERT, D_MODEL), lambda i, e: (e, 0, 0)),
                  pl.BlockSpec((1, D_MODEL), lambda i, e: (0, 0))],
        out_specs=pl.BlockSpec((tm, D_MODEL), lambda i, e: (i, 0)),
        out_shape=jax.ShapeDtypeStruct((t, D_MODEL), f32),
        compiler_params=_cparams(("parallel", "arbitrary")),
        name="moe",
    )(tb, gates, x1, wg, wu, wd, nf)


def _to_bd(s):
    bsz = s.shape[0]
    eye = jnp.eye(N_HEADS, dtype=s.dtype)
    return jnp.einsum("bhij,hg->bhigj", s, eye).reshape(bsz, GW, GW)


def _from_bd(s):
    bsz = s.shape[0]
    s5 = s.reshape(bsz, N_HEADS, HEAD_DIM, N_HEADS, HEAD_DIM)
    return jnp.stack([s5[:, h, :, h, :] for h in range(N_HEADS)], axis=1)


def _pad_hist(buf):
    return jnp.pad(buf, ((0, 0), (5, 0), (0, 0)))


def _rows8(*rows):
    out = [jnp.reshape(r, (1, -1)).astype(f32) for r in rows]
    width = out[0].shape[1]
    out += [jnp.zeros((1, width), f32)] * (8 - len(out))
    return jnp.concatenate(out, axis=0)


def _rep_head(x):
    return jnp.repeat(x, HEAD_DIM)


def _block_diag4(w):
    eye = jnp.eye(N_HEADS, dtype=w.dtype)
    return jnp.einsum("hij,hg->higj", w, eye).reshape(GW, GW)


def _rope_tables(pos):
    half = HEAD_DIM // 2
    inv = ROPE_BASE ** (-jnp.arange(half, dtype=f32) / half)
    ang = pos.astype(f32)[:, None] * inv
    cos, sin = jnp.cos(ang), jnp.sin(ang)
    cos_h = jnp.concatenate([cos, cos], axis=-1)
    sin_h = jnp.concatenate([-sin, sin], axis=-1)
    return jnp.tile(cos_h, (1, N_HEADS)), jnp.tile(sin_h, (1, N_HEADS))


def _ret_tables():
    log_gamma = jnp.log1p(-jnp.exp2(-5.0 - jnp.arange(N_HEADS, dtype=f32)))
    lg = _rep_head(log_gamma)[None, :]
    g = jnp.cumsum(jnp.broadcast_to(lg, (CHUNK, GW)), axis=0)
    g_h = g[:, ::HEAD_DIM]
    diff = g_h[:, None, :] - g_h[None, :, :]
    incl = jnp.tril(jnp.ones((CHUNK, CHUNK), dtype=bool))[:, :, None]
    dec = jnp.exp(jnp.where(incl, diff, -jnp.inf))
    dec = jnp.transpose(dec, (0, 2, 1)).reshape(CHUNK, GW)
    qgs = jnp.exp(g)
    kgs = jnp.exp(g[-1:] - g)
    gl = jnp.exp(g[-1:])
    return jnp.stack([dec, qgs, kgs]), gl


def _layer_params(l, p):
    f = lambda a: a[l]
    w_in = f(p["w_in"])
    c = [0, 768, 1024, 1028, 1032, 2056, 2312, 2568, 3336, 3592]
    a_qkv, a_z, a_b, a_a, b_rw, c_x, c_g, d_qkv, d_g = (w_in[:, c[i]:c[i + 1]] for i in range(9))
    misc = jnp.concatenate([a_b, a_a, jnp.zeros((D_MODEL, 120), f32)], axis=1)
    w_all = jnp.concatenate([a_qkv, a_z, misc, b_rw, c_x, c_g, d_qkv, d_g], axis=1).astype(bf16)
    lp = dict(
        norm_mix_w=f(p["norm_mix_w"])[None, :],
        w_all=w_all,
        gdn_cw=f(p["gdn_conv_w"]),
        gdn_vec=_rows8(-jnp.exp(_rep_head(f(p["gdn_a_log"]))), _rep_head(f(p["gdn_dt_bias"])),
                       jnp.tile(f(p["gdn_norm_w"]), N_HEADS)),
        rwkv_vec=_rows8(f(p["rwkv_w0"]), f(p["rwkv_a0"]), f(p["rwkv_k_k"]), f(p["rwkv_k_a"]),
                        f(p["rwkv_r_k"]).reshape(-1), f(p["rwkv_ln_w"]), f(p["rwkv_ln_b"])),
        rwkv_mu=f(p["rwkv_mu"])[None, :],
        rwkv_wup=f(p["rwkv_w_up"]), rwkv_aup=f(p["rwkv_a_up"]), rwkv_gup=f(p["rwkv_g_up"]),
        lru_cw=f(p["lru_conv_w"]),
        lru_vec=_rows8(f(p["lru_conv_b"]), f(p["lru_ba"]), f(p["lru_bx"]),
                       jax.nn.softplus(-f(p["lru_lambda"]))),
        lru_wa=_block_diag4(f(p["lru_wa"])).astype(bf16),
        lru_wx=_block_diag4(f(p["lru_wx"])).astype(bf16),
        ret_norm=(f(p["ret_norm_w"]), f(p["ret_norm_b"])),
        w_out=f(p["w_out"]).astype(bf16),
        norm_ffn_w=f(p["norm_ffn_w"])[None, :],
        w_router=jnp.concatenate([f(p["moe_router_e"]), f(p["moe_router_g"]),
                                  jnp.zeros((D_MODEL, 128 - N_EXPERTS - N_GROUPS), f32)], axis=1),
        b_router=jnp.concatenate([f(p["moe_router_e_b"]), f(p["moe_router_g_b"]),
                                  jnp.zeros((128 - N_EXPERTS - N_GROUPS,), f32)])[None, :],
        moe_wg=f(p["moe_w_gate"]).astype(bf16),
        moe_wu=f(p["moe_w_up"]).astype(bf16),
        moe_wd=f(p["moe_w_down"]).astype(bf16),
    )
    return lp


def _trunk(x, states, pos, layer_ps, norm_final_w):
    bsz, length, _ = x.shape
    t = bsz * length
    tm = min(512, t)
    tm_moe = min(1024, t)
    assert t % tm == 0 and t % tm_moe == 0 and bsz % SEQ_PER_STEP == 0 and length % CHUNK == 0
    cos, sin = _rope_tables(pos)
    ret_tab, ret_gl = _ret_tables()
    new = {k: [] for k in ("gdn", "gdn_conv", "rwkv", "rwkv_shift", "lru", "lru_conv", "ret")}
    x2d = x.reshape(t, D_MODEL)
    for l, lp in enumerate(layer_ps):
        pa, pb, pc, pd = _proj(x2d, lp["norm_mix_w"], lp["w_all"], tm)
        pa = pa.reshape(bsz, length, PA_W)
        pb = pb.reshape(bsz, length, PB_W)
        pc = pc.reshape(bsz, length, PC_W)
        pd = pd.reshape(bsz, length, PD_W)
        o_a, s_gdn = _gdn(pa, _pad_hist(states["gdn_conv"][l]), _to_bd(states["gdn"][l]),
                          lp["gdn_cw"], lp["gdn_vec"])
        o_b, s_rwkv = _rwkv(pb, states["rwkv_shift"][l], _to_bd(states["rwkv"][l]), lp["rwkv_vec"],
                            lp["rwkv_mu"], lp["rwkv_wup"], lp["rwkv_aup"], lp["rwkv_gup"])
        o_c, h_lru = _lru(pc, _pad_hist(states["lru_conv"][l]), states["lru"][l][:, None, :],
                          lp["lru_cw"], lp["lru_vec"], lp["lru_wa"], lp["lru_wx"])
        o_d, s_ret = _ret(pd, _to_bd(states["ret"][l]), cos, sin, ret_tab,
                          _rows8(ret_gl, lp["ret_norm"][0], lp["ret_norm"][1]))
        outs = [o.reshape(t, GW) for o in (o_a, o_b, o_c, o_d)]
        x1, tb, gates = _post(x2d, outs, lp["w_out"], lp["norm_ffn_w"], lp["w_router"], lp["b_router"], tm)
        x2d = _moe(tb, gates, x1, lp["moe_wg"], lp["moe_wu"], lp["moe_wd"], norm_final_w[None, :],
                   tm_moe, final_norm=(l == len(layer_ps) - 1))
        new["gdn"].append(_from_bd(s_gdn))
        new["gdn_conv"].append(pa[:, length - 3:, 0:3 * GW])
        new["rwkv"].append(_from_bd(s_rwkv))
        new["rwkv_shift"].append(pb[:, length - 1:, :])
        new["lru"].append(h_lru[:, 0, :])
        new["lru_conv"].append(pc[:, length - 3:, 0:GW])
        new["ret"].append(_from_bd(s_ret))
    return x2d.reshape(bsz, length, D_MODEL), {k: jnp.stack(v) for k, v in new.items()}


def _zero_states(bsz, dtype):
    return {"gdn": jnp.zeros((DEPTH, bsz, N_HEADS, HEAD_DIM, HEAD_DIM), dtype),
            "gdn_conv": jnp.zeros((DEPTH, bsz, 3, 3 * GW), dtype),
            "rwkv": jnp.zeros((DEPTH, bsz, N_HEADS, HEAD_DIM, HEAD_DIM), dtype),
            "rwkv_shift": jnp.zeros((DEPTH, bsz, 1, PB_W), dtype),
            "lru": jnp.zeros((DEPTH, bsz, GW), dtype),
            "lru_conv": jnp.zeros((DEPTH, bsz, 3, GW), dtype),
            "ret": jnp.zeros((DEPTH, bsz, N_HEADS, HEAD_DIM, HEAD_DIM), dtype)}


def kernel(x_prompt, x_sample, state_gdn, state_gdn_conv, state_rwkv, state_rwkv_shift, state_lru, state_lru_conv, state_ret, norm_mix_w, w_in, gdn_conv_w, gdn_a_log, gdn_dt_bias, gdn_norm_w, rwkv_mu, rwkv_w0, rwkv_w_up, rwkv_a0, rwkv_a_up, rwkv_g_up, rwkv_k_k, rwkv_k_a, rwkv_r_k, rwkv_ln_w, rwkv_ln_b, lru_conv_w, lru_conv_b, lru_wa, lru_ba, lru_wx, lru_bx, lru_lambda, ret_norm_w, ret_norm_b, w_out, norm_ffn_w, moe_router_g, moe_router_g_b, moe_router_e, moe_router_e_b, moe_w_gate, moe_w_up, moe_w_down, norm_final_w):
    p = dict(norm_mix_w=norm_mix_w, w_in=w_in, gdn_conv_w=gdn_conv_w, gdn_a_log=gdn_a_log,
             gdn_dt_bias=gdn_dt_bias, gdn_norm_w=gdn_norm_w, rwkv_mu=rwkv_mu, rwkv_w0=rwkv_w0,
             rwkv_w_up=rwkv_w_up, rwkv_a0=rwkv_a0, rwkv_a_up=rwkv_a_up, rwkv_g_up=rwkv_g_up,
             rwkv_k_k=rwkv_k_k, rwkv_k_a=rwkv_k_a, rwkv_r_k=rwkv_r_k, rwkv_ln_w=rwkv_ln_w,
             rwkv_ln_b=rwkv_ln_b, lru_conv_w=lru_conv_w, lru_conv_b=lru_conv_b, lru_wa=lru_wa,
             lru_ba=lru_ba, lru_wx=lru_wx, lru_bx=lru_bx, lru_lambda=lru_lambda,
             ret_norm_w=ret_norm_w, ret_norm_b=ret_norm_b, w_out=w_out, norm_ffn_w=norm_ffn_w,
             moe_router_g=moe_router_g, moe_router_g_b=moe_router_g_b, moe_router_e=moe_router_e,
             moe_router_e_b=moe_router_e_b, moe_w_gate=moe_w_gate, moe_w_up=moe_w_up,
             moe_w_down=moe_w_down)
    depth = w_in.shape[0]
    layer_ps = [_layer_params(l, p) for l in range(depth)]
    pos_p = jnp.arange(x_prompt.shape[1], dtype=jnp.int32)
    y_p, new_p = _trunk(x_prompt, _zero_states(x_prompt.shape[0], x_prompt.dtype), pos_p, layer_ps, norm_final_w)
    states_s = {"gdn": state_gdn, "gdn_conv": state_gdn_conv, "rwkv": state_rwkv,
                "rwkv_shift": state_rwkv_shift, "lru": state_lru, "lru_conv": state_lru_conv,
                "ret": state_ret}
    pos_s = PAST_LEN + jnp.arange(x_sample.shape[1], dtype=jnp.int32)
    y_s, new_s = _trunk(x_sample, states_s, pos_s, layer_ps, norm_final_w)
    return (y_p, y_s,
            new_p["gdn"], new_s["gdn"], new_p["gdn_conv"], new_s["gdn_conv"],
            new_p["rwkv"], new_s["rwkv"], new_p["rwkv_shift"], new_s["rwkv_shift"],
            new_p["lru"], new_s["lru"], new_p["lru_conv"], new_s["lru_conv"],
            new_p["ret"], new_s["ret"])
```

```python
import functools

import jax
import jax.numpy as jnp
from jax import lax
from jax.experimental import pallas as pl
from jax.experimental.pallas import tpu as pltpu

f32 = jnp.float32
bf16 = jnp.bfloat16
HIGHEST = lax.Precision.HIGHEST

D_MODEL = 1024
N_HEADS = 4
HEAD_DIM = 64
GW = N_HEADS * HEAD_DIM
CHUNK = 64
DEPTH = 2
PAST_LEN = 4096
RWKV_W_SCALE = 0.606531
RWKV_LN_EPS = 64e-5
LRU_C = 8.0
ROPE_BASE = 10000.0
N_GROUPS = 4
EXPERTS_PER_GROUP = 4
N_EXPERTS = 16
D_EXPERT = 512

PA_W = 3 * GW + GW + 2 * GW
PB_W = 1024
PC_W = 2 * GW
PD_W = 3 * GW + GW
P_ALL = PA_W + PB_W + PC_W + PD_W

VMEM_LIMIT_BYTES = 56 * 1024 * 1024
SEQ_PER_STEP = 4

NN = (((1,), (0,)), ((), ()))
NT = (((1,), (1,)), ((), ()))
TN = (((0,), (0,)), ((), ()))


def _dot(a, b, dims=NN, precision=None):
    return lax.dot_general(a, b, dims, precision=precision, preferred_element_type=f32)


def _p1(x):
    return (x.astype(bf16),)


def _p3(x):
    hi = x.astype(bf16)
    return (hi, (x - hi.astype(f32)).astype(bf16))


def _mm(ap, bp, dims=NN):
    out = _dot(ap[0], bp[0], dims)
    if len(bp) > 1:
        out = out + _dot(ap[0], bp[1], dims)
    if len(ap) > 1:
        out = out + _dot(ap[1], bp[0], dims)
    return out


def _iota(shape, dim):
    return lax.broadcasted_iota(jnp.int32, shape, dim)


def _masks(nb):
    rows = nb * CHUNK
    t = _iota((rows, GW), 0) & (CHUNK - 1)
    lane = _iota((rows, GW), 1)
    s = lane & (HEAD_DIM - 1)
    r2 = _iota((GW, GW), 0) >> 6
    c2 = _iota((GW, GW), 1) >> 6
    bd = r2 == c2
    ts = _iota((CHUNK, CHUNK), 0)
    ss = _iota((CHUNK, CHUNK), 1)
    return dict(
        t=t, lane=lane, s=s, bd=bd,
        incl=t >= s, strict=t > s,
        blk16=(t >> 4) == (s >> 4), blk32=(t >> 5) == (s >> 5),
        bones=jnp.where(bd, 1.0, 0.0).astype(bf16),
        lt=jnp.where(ts >= ss, 1.0, 0.0).astype(bf16),
    )


def _rs(x, b):
    return x[b * CHUNK:(b + 1) * CHUNK]


def _rsp(parts, b):
    return tuple(_rs(p, b) for p in parts)


def _each(nb, fn):
    return jnp.concatenate([fn(b) for b in range(nb)], axis=0)


def _bd(parts, m):
    return tuple(
        jnp.where(m["bd"], jnp.concatenate([p] * N_HEADS, axis=0), jnp.zeros((), p.dtype))
        for p in parts)


def _mm_bd(lp, rp, m, nb, dims=NN):
    return _each(nb, lambda b: _mm(_rsp(lp, b), _bd(_rsp(rp, b), m), dims))


def _segsum(x, m):
    return _mm(_p3(x), (m["bones"],))


def _p_exact(x):
    p1 = x.astype(bf16)
    r1 = x - p1.astype(f32)
    p2 = r1.astype(bf16)
    return (p1, p2, (r1 - p2.astype(f32)).astype(bf16))


def _cumsum_t(x, m, nb):
    parts = _p_exact(x)

    def one(b):
        acc = _dot(m["lt"], _rs(parts[0], b))
        for p in parts[1:]:
            acc = acc + _dot(m["lt"], _rs(p, b))
        return acc

    return _each(nb, one)


def _last_row(x, nb):
    return _each(nb, lambda b: jnp.broadcast_to(_rs(x, b)[CHUNK - 1:CHUNK], (CHUNK, x.shape[1])))


def _softplus(x):
    return jnp.maximum(x, 0.0) + jnp.log1p(jnp.exp(-jnp.abs(x)))


def _tri_inv_q(a, m, nb, prep):
    mm = lambda x, y: _mm_bd(prep(x), prep(y), m, nb)
    m1 = -jnp.where(m["blk16"], a, 0.0)
    m2 = mm(m1, m1)
    m4 = mm(m2, m2)
    m8 = mm(m4, m4)
    q = m1
    q = q + m2 + mm(q, m2)
    q = q + m4 + mm(q, m4)
    q = q + m8 + mm(q, m8)
    for e in (jnp.where(m["blk32"] & jnp.logical_not(m["blk16"]), a, 0.0),
              jnp.where(m["blk32"], 0.0, a)):
        x = e + mm(q, e)
        y = x + mm(x, q)
        q = q - y
    return q


PREP_INV = _p1
PREP = _p1


def _cparams(sem):
    return pltpu.CompilerParams(dimension_semantics=sem, vmem_limit_bytes=VMEM_LIMIT_BYTES)


def _proj_kernel(x_ref, nw_ref, w_ref, pa_ref, pb_ref, pc_ref, pd_ref):
    x = x_ref[...]
    h = (x * lax.rsqrt(jnp.mean(x * x, axis=-1, keepdims=True) + 1e-6) * nw_ref[...]).astype(bf16)
    off = 0
    for ref, width in ((pa_ref, PA_W), (pb_ref, PB_W), (pc_ref, PC_W), (pd_ref, PD_W)):
        ref[...] = _dot(h, w_ref[:, off:off + width])
        off += width


def _proj(x2d, nw, w_all, tm):
    t = x2d.shape[0]
    widths = (PA_W, PB_W, PC_W, PD_W)
    return pl.pallas_call(
        _proj_kernel,
        grid=(t // tm,),
        in_specs=[pl.BlockSpec((tm, D_MODEL), lambda i: (i, 0)),
                  pl.BlockSpec((1, D_MODEL), lambda i: (0, 0)),
                  pl.BlockSpec((D_MODEL, P_ALL), lambda i: (0, 0))],
        out_specs=[pl.BlockSpec((tm, w), lambda i: (i, 0)) for w in widths],
        out_shape=[jax.ShapeDtypeStruct((t, w), f32) for w in widths],
        compiler_params=_cparams(("parallel",)),
        name="proj",
    )(x2d, nw, w_all)


def _conv4(xb_ref, rows, cw):
    y = xb_ref[:, 5:5 + rows, :] * cw[0:1, :]
    for j in range(1, 4):
        y = y + xb_ref[:, 5 + j:5 + j + rows, :] * cw[j:j + 1, :]
    return y


def _gdn_kernel(pa_ref, cb_ref, s0_ref, cw_ref, vec_ref, o_ref, sout_ref, xb_ref, s_ref, *, nb):
    c = pl.program_id(1)
    rows = nb * CHUNK
    m = _masks(nb)

    @pl.when(c == 0)
    def _():
        s_ref[...] = s0_ref[...]
        xb_ref[:, 0:8, :] = cb_ref[...]

    vec = vec_ref[...]
    neg_exp_alog, dt_bias, norm_w = vec[0:1, :], vec[1:2, :], vec[2:3, :]

    xb_ref[:, 8:8 + CHUNK, :] = pa_ref[:, :, 0:3 * GW]
    y = _conv4(xb_ref, CHUNK, cw_ref[...])
    xb_ref[:, 0:8, :] = xb_ref[:, CHUNK:CHUNK + 8, :]
    y = y.reshape(rows, 3 * GW)
    cq = y * jax.nn.sigmoid(y)
    q, k, v = cq[:, 0:GW], cq[:, GW:2 * GW], cq[:, 2 * GW:3 * GW]
    z = pa_ref[:, :, 3 * GW:4 * GW].reshape(rows, GW)
    b_raw = pa_ref[:, :, 4 * GW:5 * GW].reshape(rows, GW)
    a_raw = pa_ref[:, :, 5 * GW:6 * GW].reshape(rows, GW)
    q = q * lax.rsqrt(_segsum(q * q, m) + 1e-6) * (HEAD_DIM ** -0.5)
    k = k * lax.rsqrt(_segsum(k * k, m) + 1e-6)
    beta = jax.nn.sigmoid(b_raw)
    g = neg_exp_alog * _softplus(a_raw + dt_bias)
    cum = _cumsum_t(jnp.concatenate([g, jnp.where(m["strict"], g, 0.0)], axis=1), m, nb)
    gc = cum[:, 0:GW]
    decay = jnp.exp(jnp.where(m["incl"], cum[:, GW:2 * GW], -jnp.inf))
    eg = jnp.exp(gc)
    glast = _last_row(gc, nb)
    kb = k * beta
    k_p = PREP(k)
    a = jnp.where(m["strict"], _mm_bd(PREP(kb), k_p, m, nb, NT) * decay, 0.0)
    qt_p = PREP_INV(_tri_inv_q(a, m, nb, PREP_INV))
    rhs_w = kb * eg
    rhs_u = v * beta
    w = rhs_w + _mm_bd(qt_p, PREP_INV(rhs_w), m, nb)
    u = rhs_u + _mm_bd(qt_p, PREP_INV(rhs_u), m, nb)
    qk = _mm_bd(PREP(q), k_p, m, nb, NT) * decay
    qg_p = PREP(q * eg)
    kg_p = PREP(k * jnp.exp(glast - gc))
    w_p = PREP(w)
    s_old = [s_ref[b] for b in range(nb)]
    s_p = [PREP(s) for s in s_old]
    vn = u - _each(nb, lambda b: _mm(_rsp(w_p, b), s_p[b]))
    vn_p = PREP(vn)
    o = _each(nb, lambda b: _mm(_rsp(qg_p, b), s_p[b])) + _mm_bd(PREP(qk), vn_p, m, nb)
    for b in range(nb):
        upd = _mm(_rsp(kg_p, b), _rsp(vn_p, b), TN)
        s_ref[b] = s_old[b] * jnp.exp(_rs(gc, b)[CHUNK - 1:CHUNK]) + jnp.where(m["bd"], upd, 0.0)
    o = o * lax.rsqrt(_segsum(o * o, m) * (1.0 / HEAD_DIM) + 1e-6) * norm_w * (z * jax.nn.sigmoid(z))
    o_ref[...] = o.reshape(nb, CHUNK, GW)

    @pl.when(c == pl.num_programs(1) - 1)
    def _():
        sout_ref[...] = s_ref[...]


def _gdn(pa, cb8, s0_bd, cw, vec):
    bsz, length, _ = pa.shape
    nb = SEQ_PER_STEP
    grid = (bsz // nb, length // CHUNK)
    return pl.pallas_call(
        functools.partial(_gdn_kernel, nb=nb),
        grid=grid,
        in_specs=[pl.BlockSpec((nb, CHUNK, PA_W), lambda i, c: (i, c, 0)),
                  pl.BlockSpec((nb, 8, 3 * GW), lambda i, c: (i, 0, 0)),
                  pl.BlockSpec((nb, GW, GW), lambda i, c: (i, 0, 0)),
                  pl.BlockSpec((4, 3 * GW), lambda i, c: (0, 0)),
                  pl.BlockSpec((8, GW), lambda i, c: (0, 0))],
        out_specs=[pl.BlockSpec((nb, CHUNK, GW), lambda i, c: (i, c, 0)),
                   pl.BlockSpec((nb, GW, GW), lambda i, c: (i, 0, 0))],
        out_shape=[jax.ShapeDtypeStruct((bsz, length, GW), f32),
                   jax.ShapeDtypeStruct((bsz, GW, GW), f32)],
        scratch_shapes=[pltpu.VMEM((nb, CHUNK + 8, 3 * GW), f32),
                        pltpu.VMEM((nb, GW, GW), f32)],
        compiler_params=_cparams(("arbitrary", "arbitrary")),
        name="gdn",
    )(pa, cb8, s0_bd, cw, vec)


def _rwkv_kernel(pb_ref, sh_ref, s0_ref, vec_ref, mu_ref, wup_ref, aup_ref, gup_ref,
                 o_ref, sout_ref, prev_ref, s_ref, *, nb):
    c = pl.program_id(1)
    rows = nb * CHUNK
    m = _masks(nb)

    @pl.when(c == 0)
    def _():
        s_ref[...] = s0_ref[...]
        prev_ref[...] = sh_ref[...]

    vec = vec_ref[...]
    w0, a0, k_k, k_a, r_k, ln_w, ln_b = (vec[i:i + 1, :] for i in range(7))

    p = pb_ref[...].reshape(rows, PB_W)
    first = (_iota((rows, PB_W), 0) & (CHUNK - 1)) == 0
    carried = _each(nb, lambda b: jnp.broadcast_to(prev_ref[b], (CHUNK, PB_W)))
    prev = jnp.where(first, carried, pltpu.roll(p, 1, 0))
    for b in range(nb):
        prev_ref[b] = _rs(p, b)[CHUNK - 1:CHUNK, :]
    xs = p + (prev - p) * mu_ref[...]
    r, k, v = xs[:, 0:GW], xs[:, GW:2 * GW], xs[:, 2 * GW:3 * GW]
    wd, ad, gd = xs[:, 768:832], xs[:, 832:896], xs[:, 896:1024]
    logw = -RWKV_W_SCALE * jax.nn.sigmoid(w0 + _mm(_p3(jnp.tanh(wd)), _p3(wup_ref[...])))
    a = jax.nn.sigmoid(a0 + _mm(_p3(ad), _p3(aup_ref[...])))
    gate = _mm(_p3(jax.nn.sigmoid(gd)), _p3(gup_ref[...]))
    kk = k * k_k
    kk = kk * lax.rsqrt(_segsum(kk * kk, m) + 1e-6)
    k = k * (1.0 + (a - 1.0) * k_a)

    gc = _cumsum_t(logw, m, nb)
    glast = _last_row(gc, nb)
    eng = jnp.exp(-gc)
    kkd = kk * jnp.exp(gc - logw)
    bvec = a * kk
    rd = r * jnp.exp(gc)
    kkd_p, rd_p = PREP(kkd), PREP(rd)
    binv_p, kinv_p, v_p = PREP(bvec * eng), PREP(k * eng), PREP(v)

    def intra(b):
        lhs = tuple(jnp.concatenate([_rs(x, b), _rs(y, b)], axis=0) for x, y in zip(kkd_p, rd_p))
        return jnp.concatenate([_mm(lhs, _bd(_rsp(binv_p, b), m), NT),
                                _mm(lhs, _bd(_rsp(kinv_p, b), m), NT)], axis=1)

    prods = [intra(b) for b in range(nb)]
    top = jnp.concatenate([x[0:CHUNK] for x in prods], axis=0)
    bot = jnp.concatenate([x[CHUNK:] for x in prods], axis=0)
    amat = jnp.where(m["strict"], top[:, 0:GW], 0.0)
    bmat = jnp.where(m["strict"], top[:, GW:], 0.0)
    rb = jnp.where(m["incl"], bot[:, 0:GW], 0.0)
    rk = jnp.where(m["incl"], bot[:, GW:], 0.0)
    qt_p = PREP_INV(_tri_inv_q(amat, m, nb, PREP_INV))

    s_old = [s_ref[b] for b in range(nb)]
    s_p = [PREP(s) for s in s_old]
    rhs = _each(nb, lambda b: _mm(_rsp(kkd_p, b), s_p[b], NT)) + _mm_bd(PREP(bmat), v_p, m, nb)
    u = -(rhs + _mm_bd(qt_p, PREP_INV(rhs), m, nb))
    u_p = PREP(u)
    o = (_each(nb, lambda b: _mm(_rsp(rd_p, b), s_p[b], NT))
         + _mm_bd(PREP(rb), u_p, m, nb) + _mm_bd(PREP(rk), v_p, m, nb))
    dec_end = jnp.exp(glast - gc)
    bend_p, kend_p = PREP(bvec * dec_end), PREP(k * dec_end)
    for b in range(nb):
        lhs = tuple(jnp.concatenate([_rs(x, b), _rs(y, b)], axis=0) for x, y in zip(u_p, v_p))
        rhs2 = tuple(jnp.concatenate([_rs(x, b), _rs(y, b)], axis=0) for x, y in zip(bend_p, kend_p))
        upd = _mm(lhs, rhs2, TN)
        s_ref[b] = s_old[b] * jnp.exp(_rs(gc, b)[CHUNK - 1:CHUNK]) + jnp.where(m["bd"], upd, 0.0)

    mean = _segsum(o, m) * (1.0 / HEAD_DIM)
    oc = o - mean
    var = _segsum(oc * oc, m) * (1.0 / HEAD_DIM)
    on = oc * lax.rsqrt(var + RWKV_LN_EPS) * ln_w + ln_b
    bonus = _segsum(r * k * r_k, m) * v
    o_ref[...] = ((on + bonus) * gate).reshape(nb, CHUNK, GW)

    @pl.when(c == pl.num_programs(1) - 1)
    def _():
        sout_ref[...] = s_ref[...]


def _rwkv(pb, shift, s0_bd, vec, mu, wup, aup, gup):
    bsz, length, _ = pb.shape
    nb = SEQ_PER_STEP
    grid = (bsz // nb, length // CHUNK)
    full = lambda shape: pl.BlockSpec(shape, lambda i, c: tuple(0 for _ in shape))
    return pl.pallas_call(
        functools.partial(_rwkv_kernel, nb=nb),
        grid=grid,
        in_specs=[pl.BlockSpec((nb, CHUNK, PB_W), lambda i, c: (i, c, 0)),
                  pl.BlockSpec((nb, 1, PB_W), lambda i, c: (i, 0, 0)),
                  pl.BlockSpec((nb, GW, GW), lambda i, c: (i, 0, 0)),
                  full((8, GW)), full((1, PB_W)), full((64, GW)), full((64, GW)), full((128, GW))],
        out_specs=[pl.BlockSpec((nb, CHUNK, GW), lambda i, c: (i, c, 0)),
                   pl.BlockSpec((nb, GW, GW), lambda i, c: (i, 0, 0))],
        out_shape=[jax.ShapeDtypeStruct((bsz, length, GW), f32),
                   jax.ShapeDtypeStruct((bsz, GW, GW), f32)],
        scratch_shapes=[pltpu.VMEM((nb, 1, PB_W), f32),
                        pltpu.VMEM((nb, GW, GW), f32)],
        compiler_params=_cparams(("arbitrary", "arbitrary")),
        name="rwkv",
    )(pb, shift, s0_bd, vec, mu, wup, aup, gup)


def _neg_expm1(x):
    u = jnp.exp(x)
    um1 = u - 1.0
    lu = jnp.log(u)
    safe = jnp.where(um1 == 0.0, x, um1 * x / jnp.where(lu == 0.0, 1.0, lu))
    return -jnp.where(x < -0.5, um1, safe)


def _lru_kernel(pc_ref, cb_ref, h0_ref, cw_ref, vec_ref, wa_ref, wx_ref,
                o_ref, hout_ref, xb_ref, h_ref, *, nb, rows):
    c = pl.program_id(1)

    @pl.when(c == 0)
    def _():
        h_ref[...] = h0_ref[...]
        xb_ref[:, 0:8, :] = cb_ref[...]

    cw = cw_ref[...]
    vec = vec_ref[...]
    conv_b, ba, bx, sp_neg_lam = (vec[i:i + 1, :] for i in range(4))
    t = _iota((rows, GW), 0)

    xb_ref[:, 8:8 + rows, :] = pc_ref[:, :, 0:GW]
    xc_all = _conv4(xb_ref, rows, cw) + conv_b
    xb_ref[:, 0:8, :] = xb_ref[:, rows:rows + 8, :]
    for b in range(nb):
        xc = xc_all[b]
        gb = pc_ref[b, :, GW:2 * GW]
        xcp = PREP(xc)
        r = jax.nn.sigmoid(_mm(xcp, (wa_ref[...],)) + ba)
        i = jax.nn.sigmoid(_mm(xcp, (wx_ref[...],)) + bx)
        log_a = -LRU_C * r * sp_neg_lam
        a = jnp.exp(log_a)
        bt = jnp.sqrt(_neg_expm1(2.0 * log_a)) * (i * xc)
        d = 1
        while d < rows:
            keep = t >= d
            a_sh = jnp.where(keep, pltpu.roll(a, d, 0), 1.0)
            b_sh = jnp.where(keep, pltpu.roll(bt, d, 0), 0.0)
            bt = a * b_sh + bt
            a = a * a_sh
            d *= 2
        h = bt + a * h_ref[b]
        h_ref[b] = h[rows - 1:rows, :]
        o_ref[b] = h * jax.nn.gelu(gb)

    @pl.when(c == pl.num_programs(1) - 1)
    def _():
        hout_ref[...] = h_ref[...]


def _lru(pc, cb8, h0, cw, vec, wa_bd, wx_bd):
    bsz, length, _ = pc.shape
    nb = SEQ_PER_STEP
    rows = min(length, 256)
    grid = (bsz // nb, length // rows)
    full = lambda shape: pl.BlockSpec(shape, lambda i, c: tuple(0 for _ in shape))
    return pl.pallas_call(
        functools.partial(_lru_kernel, nb=nb, rows=rows),
        grid=grid,
        in_specs=[pl.BlockSpec((nb, rows, PC_W), lambda i, c: (i, c, 0)),
                  pl.BlockSpec((nb, 8, GW), lambda i, c: (i, 0, 0)),
                  pl.BlockSpec((nb, 1, GW), lambda i, c: (i, 0, 0)),
                  full((4, GW)), full((8, GW)), full((GW, GW)), full((GW, GW))],
        out_specs=[pl.BlockSpec((nb, rows, GW), lambda i, c: (i, c, 0)),
                   pl.BlockSpec((nb, 1, GW), lambda i, c: (i, 0, 0))],
        out_shape=[jax.ShapeDtypeStruct((bsz, length, GW), f32),
                   jax.ShapeDtypeStruct((bsz, 1, GW), f32)],
        scratch_shapes=[pltpu.VMEM((nb, rows + 8, GW), f32),
                        pltpu.VMEM((nb, 1, GW), f32)],
        compiler_params=_cparams(("arbitrary", "arbitrary")),
        name="lru",
    )(pc, cb8, h0, cw, vec, wa_bd, wx_bd)


def _ret_kernel(pd_ref, r0_ref, cos_ref, sin_ref, tab_ref, vec_ref, o_ref, rout_ref, s_ref, *, nb):
    c = pl.program_id(1)
    rows = nb * CHUNK
    m = _masks(nb)

    @pl.when(c == 0)
    def _():
        s_ref[...] = r0_ref[...]

    tile = lambda x: jnp.concatenate([x] * nb, axis=0)
    cos, sin = tile(cos_ref[...]), tile(sin_ref[...])
    dec, qgs, kgs = tile(tab_ref[0]), tile(tab_ref[1]), tile(tab_ref[2])
    vec = vec_ref[...]
    gl, norm_w, norm_b = vec[0:1, :], vec[1:2, :], vec[2:3, :]
    low_half = m["s"] < (HEAD_DIM // 2)

    def rot(x):
        swapped = jnp.where(low_half, pltpu.roll(x, GW - HEAD_DIM // 2, 1), pltpu.roll(x, HEAD_DIM // 2, 1))
        return x * cos + swapped * sin

    q = rot(pd_ref[:, :, 0:GW].reshape(rows, GW))
    k = rot(pd_ref[:, :, GW:2 * GW].reshape(rows, GW)) * (HEAD_DIM ** -0.5)
    v = pd_ref[:, :, 2 * GW:3 * GW].reshape(rows, GW)
    gate = pd_ref[:, :, 3 * GW:4 * GW].reshape(rows, GW)
    v_p = PREP(v)
    qk = _mm_bd(PREP(q), PREP(k), m, nb, NT) * dec
    qg_p, kg_p = PREP(q * qgs), PREP(k * kgs)
    s_old = [s_ref[b] for b in range(nb)]
    o = _each(nb, lambda b: _mm(_rsp(qg_p, b), PREP(s_old[b]))) + _mm_bd(PREP(qk), v_p, m, nb)
    for b in range(nb):
        s_ref[b] = s_old[b] * gl + jnp.where(m["bd"], _mm(_rsp(kg_p, b), _rsp(v_p, b), TN), 0.0)
    mean = _segsum(o, m) * (1.0 / HEAD_DIM)
    oc = o - mean
    var = _segsum(oc * oc, m) * (1.0 / HEAD_DIM)
    on = oc * lax.rsqrt(var + 1e-5) * norm_w + norm_b
    o_ref[...] = (gate * jax.nn.sigmoid(gate) * on).reshape(nb, CHUNK, GW)

    @pl.when(c == pl.num_programs(1) - 1)
    def _():
        rout_ref[...] = s_ref[...]


def _ret(pd, r0_bd, cos, sin, tab, vec):
    bsz, length, _ = pd.shape
    nb = SEQ_PER_STEP
    grid = (bsz // nb, length // CHUNK)
    full = lambda shape: pl.BlockSpec(shape, lambda i, c: tuple(0 for _ in shape))
    return pl.pallas_call(
        functools.partial(_ret_kernel, nb=nb),
        grid=grid,
        in_specs=[pl.BlockSpec((nb, CHUNK, PD_W), lambda i, c: (i, c, 0)),
                  pl.BlockSpec((nb, GW, GW), lambda i, c: (i, 0, 0)),
                  pl.BlockSpec((CHUNK, GW), lambda i, c: (c, 0)),
                  pl.BlockSpec((CHUNK, GW), lambda i, c: (c, 0)),
                  full((3, CHUNK, GW)), full((8, GW))],
        out_specs=[pl.BlockSpec((nb, CHUNK, GW), lambda i, c: (i, c, 0)),
                   pl.BlockSpec((nb, GW, GW), lambda i, c: (i, 0, 0))],
        out_shape=[jax.ShapeDtypeStruct((bsz, length, GW), f32),
                   jax.ShapeDtypeStruct((bsz, GW, GW), f32)],
        scratch_shapes=[pltpu.VMEM((nb, GW, GW), f32)],
        compiler_params=_cparams(("arbitrary", "arbitrary")),
        name="ret",
    )(pd, r0_bd, cos, sin, tab, vec)


def _post_kernel(x_ref, oa_ref, ob_ref, oc_ref, od_ref, wout_ref, nw_ref, wr_ref, br_ref,
                 x1_ref, t_ref, gates_ref):
    acc = x_ref[...]
    for j, ref in enumerate((oa_ref, ob_ref, oc_ref, od_ref)):
        acc = acc + _dot(ref[...].astype(bf16), wout_ref[j * GW:(j + 1) * GW, :])
    x1_ref[...] = acc
    tn = acc * lax.rsqrt(jnp.mean(acc * acc, axis=-1, keepdims=True) + 1e-6) * nw_ref[...]
    t_ref[...] = tn.astype(bf16)
    logits = _dot(tn, wr_ref[...], precision=HIGHEST) + br_ref[...]
    lane = _iota(logits.shape, 1)
    ninf = -jnp.inf
    is_g = (lane >= N_EXPERTS) & (lane < N_EXPERTS + N_GROUPS)
    lg = jnp.where(is_g, logits, ninf)
    gmax = jnp.max(lg, axis=-1, keepdims=True)
    grp = jnp.min(jnp.where(lg == gmax, lane, 1 << 20), axis=-1, keepdims=True) - N_EXPERTS
    p_grp = 1.0 / jnp.sum(jnp.exp(lg - gmax), axis=-1, keepdims=True)
    in_grp = (lane >> 2) == grp
    le = jnp.where(in_grp, logits, ninf)
    v1 = jnp.max(le, axis=-1, keepdims=True)
    i1 = jnp.min(jnp.where(le == v1, lane, 1 << 20), axis=-1, keepdims=True)
    le2 = jnp.where(lane == i1, ninf, le)
    v2 = jnp.max(le2, axis=-1, keepdims=True)
    i2 = jnp.min(jnp.where(le2 == v2, lane, 1 << 20), axis=-1, keepdims=True)
    e2 = jnp.exp(v2 - v1)
    wt1 = p_grp / (1.0 + e2)
    wt2 = p_grp * e2 / (1.0 + e2)
    gates_ref[...] = jnp.where(lane == i1, wt1, 0.0) + jnp.where(lane == i2, wt2, 0.0)


def _post(x2d, outs, wout, nw, wr, br, tm):
    t = x2d.shape[0]
    row = lambda w: pl.BlockSpec((tm, w), lambda i: (i, 0))
    full = lambda shape: pl.BlockSpec(shape, lambda i: tuple(0 for _ in shape))
    return pl.pallas_call(
        _post_kernel,
        grid=(t // tm,),
        in_specs=[row(D_MODEL), row(GW), row(GW), row(GW), row(GW),
                  full((D_MODEL, D_MODEL)), full((1, D_MODEL)), full((D_MODEL, 128)), full((1, 128))],
        out_specs=[row(D_MODEL), row(D_MODEL), row(128)],
        out_shape=[jax.ShapeDtypeStruct((t, D_MODEL), f32),
                   jax.ShapeDtypeStruct((t, D_MODEL), bf16),
                   jax.ShapeDtypeStruct((t, 128), f32)],
        compiler_params=_cparams(("parallel",)),
        name="post",
    )(x2d, *outs, wout, nw, wr, br)


def _moe_kernel(t_ref, g_ref, x1_ref, wg_ref, wu_ref, wd_ref, nf_ref, o_ref, *, final_norm):
    e = pl.program_id(1)

    @pl.when(e == 0)
    def _():
        o_ref[...] = x1_ref[...]

    tb = t_ref[...]
    hg = _dot(tb, wg_ref[0])
    hu = _dot(tb, wu_ref[0])
    he = (hg * jax.nn.sigmoid(hg) * hu).astype(bf16)
    y = _dot(he, wd_ref[0])
    gates = g_ref[...]
    lane = _iota(gates.shape, 1)
    ge = jnp.sum(jnp.where(lane == e, gates, 0.0), axis=-1, keepdims=True)
    o_ref[...] += ge * y

    if final_norm:
        @pl.when(e == pl.num_programs(1) - 1)
        def _():
            x = o_ref[...]
            o_ref[...] = x * lax.rsqrt(jnp.mean(x * x, axis=-1, keepdims=True) + 1e-6) * nf_ref[...]


def _moe(tb, gates, x1, wg, wu, wd, nf, tm, final_norm):
    t = tb.shape[0]
    return pl.pallas_call(
        functools.partial(_moe_kernel, final_norm=final_norm),
        grid=(t // tm, N_EXPERTS),
        in_specs=[pl.BlockSpec((tm, D_MODEL), lambda i, e: (i, 0)),
                  pl.BlockSpec((tm, 128), lambda i, e: (i, 0)),
                  pl.BlockSpec((tm, D_MODEL), lambda i, e: (i, 0)),
                  pl.BlockSpec((1, D_MODEL, D_EXPERT), lambda i, e: (e, 0, 0)),
                  pl.BlockSpec((1, D_MODEL, D_EXPERT), lambda i, e: (e, 0, 0)),
                  pl.BlockSpec((1, D_EXPERT, D_MODEL), lambda i, e: (e, 0, 0)),
                  pl.BlockSpec((1, D_MODEL), lambda i, e: (0, 0))],
        out_specs=pl.BlockSpec((tm, D_MODEL), lambda i, e: (i, 0)),
        out_shape=jax.ShapeDtypeStruct((t, D_MODEL), f32),
        compiler_params=_cparams(("parallel", "arbitrary")),
        name="moe",
    )(tb, gates, x1, wg, wu, wd, nf)


def _to_bd(s):
    bsz = s.shape[0]
    eye = jnp.eye(N_HEADS, dtype=s.dtype)
    return jnp.einsum("bhij,hg->bhigj", s, eye).reshape(bsz, GW, GW)


def _from_bd(s):
    bsz = s.shape[0]
    s5 = s.reshape(bsz, N_HEADS, HEAD_DIM, N_HEADS, HEAD_DIM)
    return jnp.stack([s5[:, h, :, h, :] for h in range(N_HEADS)], axis=1)


def _pad_hist(buf):
    return jnp.pad(buf, ((0, 0), (5, 0), (0, 0)))


def _rows8(*rows):
    out = [jnp.reshape(r, (1, -1)).astype(f32) for r in rows]
    width = out[0].shape[1]
    out += [jnp.zeros((1, width), f32)] * (8 - len(out))
    return jnp.concatenate(out, axis=0)


def _rep_head(x):
    return jnp.repeat(x, HEAD_DIM)


def _block_diag4(w):
    eye = jnp.eye(N_HEADS, dtype=w.dtype)
    return jnp.einsum("hij,hg->higj", w, eye).reshape(GW, GW)


def _rope_tables(pos):
    half = HEAD_DIM // 2
    inv = ROPE_BASE ** (-jnp.arange(half, dtype=f32) / half)
    ang = pos.astype(f32)[:, None] * inv
    cos, sin = jnp.cos(ang), jnp.sin(ang)
    cos_h = jnp.concatenate([cos, cos], axis=-1)
    sin_h = jnp.concatenate([-sin, sin], axis=-1)
    return jnp.tile(cos_h, (1, N_HEADS)), jnp.tile(sin_h, (1, N_HEADS))


def _ret_tables():
    log_gamma = jnp.log1p(-jnp.exp2(-5.0 - jnp.arange(N_HEADS, dtype=f32)))
    lg = _rep_head(log_gamma)[None, :]
    g = jnp.cumsum(jnp.broadcast_to(lg, (CHUNK, GW)), axis=0)
    g_h = g[:, ::HEAD_DIM]
    diff = g_h[:, None, :] - g_h[None, :, :]
    incl = jnp.tril(jnp.ones((CHUNK, CHUNK), dtype=bool))[:, :, None]
    dec = jnp.exp(jnp.where(incl, diff, -jnp.inf))
    dec = jnp.transpose(dec, (0, 2, 1)).reshape(CHUNK, GW)
    qgs = jnp.exp(g)
    kgs = jnp.exp(g[-1:] - g)
    gl = jnp.exp(g[-1:])
    return jnp.stack([dec, qgs, kgs]), gl


def _layer_params(l, p):
    f = lambda a: a[l]
    w_in = f(p["w_in"])
    c = [0, 768, 1024, 1028, 1032, 2056, 2312, 2568, 3336, 3592]
    a_qkv, a_z, a_b, a_a, b_rw, c_x, c_g, d_qkv, d_g = (w_in[:, c[i]:c[i + 1]] for i in range(9))
    rep = lambda w: jnp.repeat(w, HEAD_DIM, axis=1)
    w_all = jnp.concatenate([a_qkv, a_z, rep(a_b), rep(a_a), b_rw, c_x, c_g, d_qkv, d_g], axis=1).astype(bf16)
    lp = dict(
        norm_mix_w=f(p["norm_mix_w"])[None, :],
        w_all=w_all,
        gdn_cw=f(p["gdn_conv_w"]),
        gdn_vec=_rows8(-jnp.exp(_rep_head(f(p["gdn_a_log"]))), _rep_head(f(p["gdn_dt_bias"])),
                       jnp.tile(f(p["gdn_norm_w"]), N_HEADS)),
        rwkv_vec=_rows8(f(p["rwkv_w0"]), f(p["rwkv_a0"]), f(p["rwkv_k_k"]), f(p["rwkv_k_a"]),
                        f(p["rwkv_r_k"]).reshape(-1), f(p["rwkv_ln_w"]), f(p["rwkv_ln_b"])),
        rwkv_mu=f(p["rwkv_mu"])[None, :],
        rwkv_wup=f(p["rwkv_w_up"]), rwkv_aup=f(p["rwkv_a_up"]), rwkv_gup=f(p["rwkv_g_up"]),
        lru_cw=f(p["lru_conv_w"]),
        lru_vec=_rows8(f(p["lru_conv_b"]), f(p["lru_ba"]), f(p["lru_bx"]),
                       jax.nn.softplus(-f(p["lru_lambda"]))),
        lru_wa=_block_diag4(f(p["lru_wa"])).astype(bf16),
        lru_wx=_block_diag4(f(p["lru_wx"])).astype(bf16),
        ret_norm=(f(p["ret_norm_w"]), f(p["ret_norm_b"])),
        w_out=f(p["w_out"]).astype(bf16),
        norm_ffn_w=f(p["norm_ffn_w"])[None, :],
        w_router=jnp.concatenate([f(p["moe_router_e"]), f(p["moe_router_g"]),
                                  jnp.zeros((D_MODEL, 128 - N_EXPERTS - N_GROUPS), f32)], axis=1),
        b_router=jnp.concatenate([f(p["moe_router_e_b"]), f(p["moe_router_g_b"]),
                                  jnp.zeros((128 - N_EXPERTS - N_GROUPS,), f32)])[None, :],
        moe_wg=f(p["moe_w_gate"]).astype(bf16),
        moe_wu=f(p["moe_w_up"]).astype(bf16),
        moe_wd=f(p["moe_w_down"]).astype(bf16),
    )
    return lp


def _trunk(x, states, pos, layer_ps, norm_final_w):
    bsz, length, _ = x.shape
    t = bsz * length
    tm = min(512, t)
    tm_moe = min(1024, t)
    assert t % tm == 0 and t % tm_moe == 0 and bsz % SEQ_PER_STEP == 0 and length % CHUNK == 0
    cos, sin = _rope_tables(pos)
    ret_tab, ret_gl = _ret_tables()
    new = {k: [] for k in ("gdn", "gdn_conv", "rwkv", "rwkv_shift", "lru", "lru_conv", "ret")}
    x2d = x.reshape(t, D_MODEL)
    for l, lp in enumerate(layer_ps):
        pa, pb, pc, pd = _proj(x2d, lp["norm_mix_w"], lp["w_all"], tm)
        pa = pa.reshape(bsz, length, PA_W)
        pb = pb.reshape(bsz, length, PB_W)
        pc = pc.reshape(bsz, length, PC_W)
        pd = pd.reshape(bsz, length, PD_W)
        o_a, s_gdn = _gdn(pa, _pad_hist(states["gdn_conv"][l]), _to_bd(states["gdn"][l]),
                          lp["gdn_cw"], lp["gdn_vec"])
        o_b, s_rwkv = _rwkv(pb, states["rwkv_shift"][l], _to_bd(states["rwkv"][l]), lp["rwkv_vec"],
                            lp["rwkv_mu"], lp["rwkv_wup"], lp["rwkv_aup"], lp["rwkv_gup"])
        o_c, h_lru = _lru(pc, _pad_hist(states["lru_conv"][l]), states["lru"][l][:, None, :],
                          lp["lru_cw"], lp["lru_vec"], lp["lru_wa"], lp["lru_wx"])
        o_d, s_ret = _ret(pd, _to_bd(states["ret"][l]), cos, sin, ret_tab,
                          _rows8(ret_gl, lp["ret_norm"][0], lp["ret_norm"][1]))
        outs = [o.reshape(t, GW) for o in (o_a, o_b, o_c, o_d)]
        x1, tb, gates = _post(x2d, outs, lp["w_out"], lp["norm_ffn_w"], lp["w_router"], lp["b_router"], tm)
        x2d = _moe(tb, gates, x1, lp["moe_wg"], lp["moe_wu"], lp["moe_wd"], norm_final_w[None, :],
                   tm_moe, final_norm=(l == len(layer_ps) - 1))
        new["gdn"].append(_from_bd(s_gdn))
        new["gdn_conv"].append(pa[:, length - 3:, 0:3 * GW])
        new["rwkv"].append(_from_bd(s_rwkv))
        new["rwkv_shift"].append(pb[:, length - 1:, :])
        new["lru"].append(h_lru[:, 0, :])
        new["lru_conv"].append(pc[:, length - 3:, 0:GW])
        new["ret"].append(_from_bd(s_ret))
    return x2d.reshape(bsz, length, D_MODEL), {k: jnp.stack(v) for k, v in new.items()}


def _zero_states(bsz, dtype):
    return {"gdn": jnp.zeros((DEPTH, bsz, N_HEADS, HEAD_DIM, HEAD_DIM), dtype),
            "gdn_conv": jnp.zeros((DEPTH, bsz, 3, 3 * GW), dtype),
            "rwkv": jnp.zeros((DEPTH, bsz, N_HEADS, HEAD_DIM, HEAD_DIM), dtype),
            "rwkv_shift": jnp.zeros((DEPTH, bsz, 1, PB_W), dtype),
            "lru": jnp.zeros((DEPTH, bsz, GW), dtype),
            "lru_conv": jnp.zeros((DEPTH, bsz, 3, GW), dtype),
            "ret": jnp.zeros((DEPTH, bsz, N_HEADS, HEAD_DIM, HEAD_DIM), dtype)}


def kernel(x_prompt, x_sample, state_gdn, state_gdn_conv, state_rwkv, state_rwkv_shift, state_lru, state_lru_conv, state_ret, norm_mix_w, w_in, gdn_conv_w, gdn_a_log, gdn_dt_bias, gdn_norm_w, rwkv_mu, rwkv_w0, rwkv_w_up, rwkv_a0, rwkv_a_up, rwkv_g_up, rwkv_k_k, rwkv_k_a, rwkv_r_k, rwkv_ln_w, rwkv_ln_b, lru_conv_w, lru_conv_b, lru_wa, lru_ba, lru_wx, lru_bx, lru_lambda, ret_norm_w, ret_norm_b, w_out, norm_ffn_w, moe_router_g, moe_router_g_b, moe_router_e, moe_router_e_b, moe_w_gate, moe_w_up, moe_w_down, norm_final_w):
    p = dict(norm_mix_w=norm_mix_w, w_in=w_in, gdn_conv_w=gdn_conv_w, gdn_a_log=gdn_a_log,
             gdn_dt_bias=gdn_dt_bias, gdn_norm_w=gdn_norm_w, rwkv_mu=rwkv_mu, rwkv_w0=rwkv_w0,
             rwkv_w_up=rwkv_w_up, rwkv_a0=rwkv_a0, rwkv_a_up=rwkv_a_up, rwkv_g_up=rwkv_g_up,
             rwkv_k_k=rwkv_k_k, rwkv_k_a=rwkv_k_a, rwkv_r_k=rwkv_r_k, rwkv_ln_w=rwkv_ln_w,
             rwkv_ln_b=rwkv_ln_b, lru_conv_w=lru_conv_w, lru_conv_b=lru_conv_b, lru_wa=lru_wa,
             lru_ba=lru_ba, lru_wx=lru_wx, lru_bx=lru_bx, lru_lambda=lru_lambda,
             ret_norm_w=ret_norm_w, ret_norm_b=ret_norm_b, w_out=w_out, norm_ffn_w=norm_ffn_w,
             moe_router_g=moe_router_g, moe_router_g_b=moe_router_g_b, moe_router_e=moe_router_e,
             moe_router_e_b=moe_router_e_b, moe_w_gate=moe_w_gate, moe_w_up=moe_w_up,
             moe_w_down=moe_w_down)
    depth = w_in.shape[0]
    layer_ps = [_layer_params(l, p) for l in range(depth)]
    pos_p = jnp.arange(x_prompt.shape[1], dtype=jnp.int32)
    y_p, new_p = _trunk(x_prompt, _zero_states(x_prompt.shape[0], x_prompt.dtype), pos_p, layer_ps, norm_final_w)
    states_s = {"gdn": state_gdn, "gdn_conv": state_gdn_conv, "rwkv": state_rwkv,
                "rwkv_shift": state_rwkv_shift, "lru": state_lru, "lru_conv": state_lru_conv,
                "ret": state_ret}
    pos_s = PAST_LEN + jnp.arange(x_sample.shape[1], dtype=jnp.int32)
    y_s, new_s = _trunk(x_sample, states_s, pos_s, layer_ps, norm_final_w)
    return (y_p, y_s,
            new_p["gdn"], new_s["gdn"], new_p["gdn_conv"], new_s["gdn_conv"],
            new_p["rwkv"], new_s["rwkv"], new_p["rwkv_shift"], new_s["rwkv_shift"],
            new_p["lru"], new_s["lru"], new_p["lru_conv"], new_s["lru_conv"],
            new_p["ret"], new_s["ret"])
```

```python
import functools

import jax
import jax.numpy as jnp
from jax import lax
from jax.experimental import pallas as pl
from jax.experimental.pallas import tpu as pltpu

f32 = jnp.float32
bf16 = jnp.bfloat16
HIGHEST = lax.Precision.HIGHEST

D_MODEL = 1024
N_HEADS = 4
HEAD_DIM = 64
GW = N_HEADS * HEAD_DIM
CHUNK = 64
DEPTH = 2
PAST_LEN = 4096
RWKV_W_SCALE = 0.606531
RWKV_LN_EPS = 64e-5
LRU_C = 8.0
ROPE_BASE = 10000.0
N_GROUPS = 4
EXPERTS_PER_GROUP = 4
N_EXPERTS = 16
D_EXPERT = 512

PA_W = 3 * GW + GW + 2 * GW
PB_W = 1024
PC_W = 2 * GW
PD_W = 3 * GW + GW
P_ALL = PA_W + PB_W + PC_W + PD_W

XG_W = D_MODEL + 128
GRP_LANE = 16

VMEM_LIMIT_BYTES = 56 * 1024 * 1024
SEQ_PER_STEP = 4

NN = (((1,), (0,)), ((), ()))
NT = (((1,), (1,)), ((), ()))
TN = (((0,), (0,)), ((), ()))


def _dot(a, b, dims=NN, precision=None):
    return lax.dot_general(a, b, dims, precision=precision, preferred_element_type=f32)


def _p1(x):
    return (x.astype(bf16),)


def _p3(x):
    hi = x.astype(bf16)
    return (hi, (x - hi.astype(f32)).astype(bf16))


def _mm(ap, bp, dims=NN):
    out = _dot(ap[0], bp[0], dims)
    if len(bp) > 1:
        out = out + _dot(ap[0], bp[1], dims)
    if len(ap) > 1:
        out = out + _dot(ap[1], bp[0], dims)
    return out


def _iota(shape, dim):
    return lax.broadcasted_iota(jnp.int32, shape, dim)


def _masks(nb):
    rows = nb * CHUNK
    t = _iota((rows, GW), 0) & (CHUNK - 1)
    lane = _iota((rows, GW), 1)
    s = lane & (HEAD_DIM - 1)
    r2 = _iota((GW, GW), 0) >> 6
    c2 = _iota((GW, GW), 1) >> 6
    bd = r2 == c2
    ts = _iota((CHUNK, CHUNK), 0)
    ss = _iota((CHUNK, CHUNK), 1)
    return dict(
        t=t, lane=lane, s=s, bd=bd,
        incl=t >= s, strict=t > s,
        blk16=(t >> 4) == (s >> 4), blk32=(t >> 5) == (s >> 5),
        bones=jnp.where(bd, 1.0, 0.0).astype(bf16),
        lt=jnp.where(ts >= ss, 1.0, 0.0).astype(bf16),
    )


def _rs(x, b):
    return x[b * CHUNK:(b + 1) * CHUNK]


def _rsp(parts, b):
    return tuple(_rs(p, b) for p in parts)


def _each(nb, fn):
    return jnp.concatenate([fn(b) for b in range(nb)], axis=0)


def _bd(parts, m):
    return tuple(
        jnp.where(m["bd"], jnp.concatenate([p] * N_HEADS, axis=0), jnp.zeros((), p.dtype))
        for p in parts)


def _mm_bd(lp, rp, m, nb, dims=NN):
    return _each(nb, lambda b: _mm(_rsp(lp, b), _bd(_rsp(rp, b), m), dims))


def _segsum(x, m):
    return _mm(_p3(x), (m["bones"],))


def _p_exact(x):
    p1 = x.astype(bf16)
    r1 = x - p1.astype(f32)
    p2 = r1.astype(bf16)
    return (p1, p2, (r1 - p2.astype(f32)).astype(bf16))


def _cumsum_t(x, m, nb):
    parts = _p_exact(x)

    def one(b):
        acc = _dot(m["lt"], _rs(parts[0], b))
        for p in parts[1:]:
            acc = acc + _dot(m["lt"], _rs(p, b))
        return acc

    return _each(nb, one)


def _last_row(x, nb):
    return _each(nb, lambda b: jnp.broadcast_to(_rs(x, b)[CHUNK - 1:CHUNK], (CHUNK, x.shape[1])))


def _softplus(x):
    return jnp.maximum(x, 0.0) + jnp.log1p(jnp.exp(-jnp.abs(x)))


def _tri_inv_q(a, m, nb, prep):
    mm = lambda x, y: _mm_bd(prep(x), prep(y), m, nb)
    m1 = -jnp.where(m["blk16"], a, 0.0)
    m2 = mm(m1, m1)
    m4 = mm(m2, m2)
    m8 = mm(m4, m4)
    q = m1
    q = q + m2 + mm(q, m2)
    q = q + m4 + mm(q, m4)
    q = q + m8 + mm(q, m8)
    for e in (jnp.where(m["blk32"] & jnp.logical_not(m["blk16"]), a, 0.0),
              jnp.where(m["blk32"], 0.0, a)):
        x = e + mm(q, e)
        y = x + mm(x, q)
        q = q - y
    return q


PREP_INV = _p1
PREP = _p1


def _cparams(sem):
    return pltpu.CompilerParams(dimension_semantics=sem, vmem_limit_bytes=VMEM_LIMIT_BYTES)


def _proj_kernel(x_ref, nw_ref, w_ref, pa_ref, pb_ref, pc_ref, pd_ref):
    x = x_ref[...]
    h = (x * lax.rsqrt(jnp.mean(x * x, axis=-1, keepdims=True) + 1e-6) * nw_ref[...]).astype(bf16)
    off = 0
    for ref, width in ((pa_ref, PA_W), (pb_ref, PB_W), (pc_ref, PC_W), (pd_ref, PD_W)):
        ref[...] = _dot(h, w_ref[:, off:off + width])
        off += width


def _proj(x2d, nw, w_all, tm):
    t = x2d.shape[0]
    widths = (PA_W, PB_W, PC_W, PD_W)
    return pl.pallas_call(
        _proj_kernel,
        grid=(t // tm,),
        in_specs=[pl.BlockSpec((tm, D_MODEL), lambda i: (i, 0)),
                  pl.BlockSpec((1, D_MODEL), lambda i: (0, 0)),
                  pl.BlockSpec((D_MODEL, P_ALL), lambda i: (0, 0))],
        out_specs=[pl.BlockSpec((tm, w), lambda i: (i, 0)) for w in widths],
        out_shape=[jax.ShapeDtypeStruct((t, w), f32) for w in widths],
        compiler_params=_cparams(("parallel",)),
        name="proj",
    )(x2d, nw, w_all)


def _conv4(xb_ref, rows, cw):
    y = xb_ref[:, 5:5 + rows, :] * cw[0:1, :]
    for j in range(1, 4):
        y = y + xb_ref[:, 5 + j:5 + j + rows, :] * cw[j:j + 1, :]
    return y


def _gdn_kernel(pa_ref, cb_ref, s0_ref, cw_ref, vec_ref, o_ref, sout_ref, xb_ref, s_ref, *, nb):
    c = pl.program_id(1)
    rows = nb * CHUNK
    m = _masks(nb)

    @pl.when(c == 0)
    def _():
        s_ref[...] = s0_ref[...]
        xb_ref[:, 0:8, :] = cb_ref[...]

    vec = vec_ref[...]
    neg_exp_alog, dt_bias, norm_w = vec[0:1, :], vec[1:2, :], vec[2:3, :]

    xb_ref[:, 8:8 + CHUNK, :] = pa_ref[:, :, 0:3 * GW]
    y = _conv4(xb_ref, CHUNK, cw_ref[...])
    xb_ref[:, 0:8, :] = xb_ref[:, CHUNK:CHUNK + 8, :]
    y = y.reshape(rows, 3 * GW)
    cq = y * jax.nn.sigmoid(y)
    q, k, v = cq[:, 0:GW], cq[:, GW:2 * GW], cq[:, 2 * GW:3 * GW]
    z = pa_ref[:, :, 3 * GW:4 * GW].reshape(rows, GW)
    b_raw = pa_ref[:, :, 4 * GW:5 * GW].reshape(rows, GW)
    a_raw = pa_ref[:, :, 5 * GW:6 * GW].reshape(rows, GW)
    q = q * lax.rsqrt(_segsum(q * q, m) + 1e-6) * (HEAD_DIM ** -0.5)
    k = k * lax.rsqrt(_segsum(k * k, m) + 1e-6)
    beta = jax.nn.sigmoid(b_raw)
    g = neg_exp_alog * _softplus(a_raw + dt_bias)
    cum = _cumsum_t(jnp.concatenate([g, jnp.where(m["strict"], g, 0.0)], axis=1), m, nb)
    gc = cum[:, 0:GW]
    decay = jnp.exp(jnp.where(m["incl"], cum[:, GW:2 * GW], -jnp.inf))
    eg = jnp.exp(gc)
    glast = _last_row(gc, nb)
    kb = k * beta
    k_p = PREP(k)
    a = jnp.where(m["strict"], _mm_bd(PREP(kb), k_p, m, nb, NT) * decay, 0.0)
    qt_p = PREP_INV(_tri_inv_q(a, m, nb, PREP_INV))
    rhs_w = kb * eg
    rhs_u = v * beta
    w = rhs_w + _mm_bd(qt_p, PREP_INV(rhs_w), m, nb)
    u = rhs_u + _mm_bd(qt_p, PREP_INV(rhs_u), m, nb)
    qk = _mm_bd(PREP(q), k_p, m, nb, NT) * decay
    qg_p = PREP(q * eg)
    kg_p = PREP(k * jnp.exp(glast - gc))
    w_p = PREP(w)
    s_old = [s_ref[b] for b in range(nb)]
    s_p = [PREP(s) for s in s_old]
    vn = u - _each(nb, lambda b: _mm(_rsp(w_p, b), s_p[b]))
    vn_p = PREP(vn)
    o = _each(nb, lambda b: _mm(_rsp(qg_p, b), s_p[b])) + _mm_bd(PREP(qk), vn_p, m, nb)
    for b in range(nb):
        upd = _mm(_rsp(kg_p, b), _rsp(vn_p, b), TN)
        s_ref[b] = s_old[b] * jnp.exp(_rs(gc, b)[CHUNK - 1:CHUNK]) + jnp.where(m["bd"], upd, 0.0)
    o = o * lax.rsqrt(_segsum(o * o, m) * (1.0 / HEAD_DIM) + 1e-6) * norm_w * (z * jax.nn.sigmoid(z))
    o_ref[...] = o.reshape(nb, CHUNK, GW)

    @pl.when(c == pl.num_programs(1) - 1)
    def _():
        sout_ref[...] = s_ref[...]


def _gdn(pa, cb8, s0_bd, cw, vec):
    bsz, length, _ = pa.shape
    nb = SEQ_PER_STEP
    grid = (bsz // nb, length // CHUNK)
    return pl.pallas_call(
        functools.partial(_gdn_kernel, nb=nb),
        grid=grid,
        in_specs=[pl.BlockSpec((nb, CHUNK, PA_W), lambda i, c: (i, c, 0)),
                  pl.BlockSpec((nb, 8, 3 * GW), lambda i, c: (i, 0, 0)),
                  pl.BlockSpec((nb, GW, GW), lambda i, c: (i, 0, 0)),
                  pl.BlockSpec((4, 3 * GW), lambda i, c: (0, 0)),
                  pl.BlockSpec((8, GW), lambda i, c: (0, 0))],
        out_specs=[pl.BlockSpec((nb, CHUNK, GW), lambda i, c: (i, c, 0)),
                   pl.BlockSpec((nb, GW, GW), lambda i, c: (i, 0, 0))],
        out_shape=[jax.ShapeDtypeStruct((bsz, length, GW), f32),
                   jax.ShapeDtypeStruct((bsz, GW, GW), f32)],
        scratch_shapes=[pltpu.VMEM((nb, CHUNK + 8, 3 * GW), f32),
                        pltpu.VMEM((nb, GW, GW), f32)],
        compiler_params=_cparams(("arbitrary", "arbitrary")),
        name="gdn",
    )(pa, cb8, s0_bd, cw, vec)


def _rwkv_kernel(pb_ref, sh_ref, s0_ref, vec_ref, mu_ref, wup_ref, aup_ref, gup_ref,
                 o_ref, sout_ref, prev_ref, s_ref, *, nb):
    c = pl.program_id(1)
    rows = nb * CHUNK
    m = _masks(nb)

    @pl.when(c == 0)
    def _():
        s_ref[...] = s0_ref[...]
        prev_ref[...] = sh_ref[...]

    vec = vec_ref[...]
    w0, a0, k_k, k_a, r_k, ln_w, ln_b = (vec[i:i + 1, :] for i in range(7))

    p = pb_ref[...].reshape(rows, PB_W)
    first = (_iota((rows, PB_W), 0) & (CHUNK - 1)) == 0
    carried = _each(nb, lambda b: jnp.broadcast_to(prev_ref[b], (CHUNK, PB_W)))
    prev = jnp.where(first, carried, pltpu.roll(p, 1, 0))
    for b in range(nb):
        prev_ref[b] = _rs(p, b)[CHUNK - 1:CHUNK, :]
    xs = p + (prev - p) * mu_ref[...]
    r, k, v = xs[:, 0:GW], xs[:, GW:2 * GW], xs[:, 2 * GW:3 * GW]
    wd, ad, gd = xs[:, 768:832], xs[:, 832:896], xs[:, 896:1024]
    logw = -RWKV_W_SCALE * jax.nn.sigmoid(w0 + _mm(_p3(jnp.tanh(wd)), _p3(wup_ref[...])))
    a = jax.nn.sigmoid(a0 + _mm(_p3(ad), _p3(aup_ref[...])))
    gate = _mm(_p3(jax.nn.sigmoid(gd)), _p3(gup_ref[...]))
    kk = k * k_k
    kk = kk * lax.rsqrt(_segsum(kk * kk, m) + 1e-6)
    k = k * (1.0 + (a - 1.0) * k_a)

    gc = _cumsum_t(logw, m, nb)
    glast = _last_row(gc, nb)
    eng = jnp.exp(-gc)
    kkd = kk * jnp.exp(gc - logw)
    bvec = a * kk
    rd = r * jnp.exp(gc)
    kkd_p, rd_p = PREP(kkd), PREP(rd)
    binv_p, kinv_p, v_p = PREP(bvec * eng), PREP(k * eng), PREP(v)

    def intra(b):
        lhs = tuple(jnp.concatenate([_rs(x, b), _rs(y, b)], axis=0) for x, y in zip(kkd_p, rd_p))
        return jnp.concatenate([_mm(lhs, _bd(_rsp(binv_p, b), m), NT),
                                _mm(lhs, _bd(_rsp(kinv_p, b), m), NT)], axis=1)

    prods = [intra(b) for b in range(nb)]
    top = jnp.concatenate([x[0:CHUNK] for x in prods], axis=0)
    bot = jnp.concatenate([x[CHUNK:] for x in prods], axis=0)
    amat = jnp.where(m["strict"], top[:, 0:GW], 0.0)
    bmat = jnp.where(m["strict"], top[:, GW:], 0.0)
    rb = jnp.where(m["incl"], bot[:, 0:GW], 0.0)
    rk = jnp.where(m["incl"], bot[:, GW:], 0.0)
    qt_p = PREP_INV(_tri_inv_q(amat, m, nb, PREP_INV))

    s_old = [s_ref[b] for b in range(nb)]
    s_p = [PREP(s) for s in s_old]
    rhs = _each(nb, lambda b: _mm(_rsp(kkd_p, b), s_p[b], NT)) + _mm_bd(PREP(bmat), v_p, m, nb)
    u = -(rhs + _mm_bd(qt_p, PREP_INV(rhs), m, nb))
    u_p = PREP(u)
    o = (_each(nb, lambda b: _mm(_rsp(rd_p, b), s_p[b], NT))
         + _mm_bd(PREP(rb), u_p, m, nb) + _mm_bd(PREP(rk), v_p, m, nb))
    dec_end = jnp.exp(glast - gc)
    bend_p, kend_p = PREP(bvec * dec_end), PREP(k * dec_end)
    for b in range(nb):
        lhs = tuple(jnp.concatenate([_rs(x, b), _rs(y, b)], axis=0) for x, y in zip(u_p, v_p))
        rhs2 = tuple(jnp.concatenate([_rs(x, b), _rs(y, b)], axis=0) for x, y in zip(bend_p, kend_p))
        upd = _mm(lhs, rhs2, TN)
        s_ref[b] = s_old[b] * jnp.exp(_rs(gc, b)[CHUNK - 1:CHUNK]) + jnp.where(m["bd"], upd, 0.0)

    mean = _segsum(o, m) * (1.0 / HEAD_DIM)
    oc = o - mean
    var = _segsum(oc * oc, m) * (1.0 / HEAD_DIM)
    on = oc * lax.rsqrt(var + RWKV_LN_EPS) * ln_w + ln_b
    bonus = _segsum(r * k * r_k, m) * v
    o_ref[...] = ((on + bonus) * gate).reshape(nb, CHUNK, GW)

    @pl.when(c == pl.num_programs(1) - 1)
    def _():
        sout_ref[...] = s_ref[...]


def _rwkv(pb, shift, s0_bd, vec, mu, wup, aup, gup):
    bsz, length, _ = pb.shape
    nb = SEQ_PER_STEP
    grid = (bsz // nb, length // CHUNK)
    full = lambda shape: pl.BlockSpec(shape, lambda i, c: tuple(0 for _ in shape))
    return pl.pallas_call(
        functools.partial(_rwkv_kernel, nb=nb),
        grid=grid,
        in_specs=[pl.BlockSpec((nb, CHUNK, PB_W), lambda i, c: (i, c, 0)),
                  pl.BlockSpec((nb, 1, PB_W), lambda i, c: (i, 0, 0)),
                  pl.BlockSpec((nb, GW, GW), lambda i, c: (i, 0, 0)),
                  full((8, GW)), full((1, PB_W)), full((64, GW)), full((64, GW)), full((128, GW))],
        out_specs=[pl.BlockSpec((nb, CHUNK, GW), lambda i, c: (i, c, 0)),
                   pl.BlockSpec((nb, GW, GW), lambda i, c: (i, 0, 0))],
        out_shape=[jax.ShapeDtypeStruct((bsz, length, GW), f32),
                   jax.ShapeDtypeStruct((bsz, GW, GW), f32)],
        scratch_shapes=[pltpu.VMEM((nb, 1, PB_W), f32),
                        pltpu.VMEM((nb, GW, GW), f32)],
        compiler_params=_cparams(("arbitrary", "arbitrary")),
        name="rwkv",
    )(pb, shift, s0_bd, vec, mu, wup, aup, gup)


def _neg_expm1(x):
    u = jnp.exp(x)
    um1 = u - 1.0
    lu = jnp.log(u)
    safe = jnp.where(um1 == 0.0, x, um1 * x / jnp.where(lu == 0.0, 1.0, lu))
    return -jnp.where(x < -0.5, um1, safe)


def _lru_kernel(pc_ref, cb_ref, h0_ref, cw_ref, vec_ref, wa_ref, wx_ref,
                o_ref, hout_ref, xb_ref, h_ref, *, nb, rows):
    c = pl.program_id(1)

    @pl.when(c == 0)
    def _():
        h_ref[...] = h0_ref[...]
        xb_ref[:, 0:8, :] = cb_ref[...]

    cw = cw_ref[...]
    vec = vec_ref[...]
    conv_b, ba, bx, sp_neg_lam = (vec[i:i + 1, :] for i in range(4))
    t = _iota((rows, GW), 0)

    xb_ref[:, 8:8 + rows, :] = pc_ref[:, :, 0:GW]
    xc_all = _conv4(xb_ref, rows, cw) + conv_b
    xb_ref[:, 0:8, :] = xb_ref[:, rows:rows + 8, :]
    for b in range(nb):
        xc = xc_all[b]
        gb = pc_ref[b, :, GW:2 * GW]
        xcp = PREP(xc)
        r = jax.nn.sigmoid(_mm(xcp, (wa_ref[...],)) + ba)
        i = jax.nn.sigmoid(_mm(xcp, (wx_ref[...],)) + bx)
        log_a = -LRU_C * r * sp_neg_lam
        a = jnp.exp(log_a)
        bt = jnp.sqrt(_neg_expm1(2.0 * log_a)) * (i * xc)
        d = 1
        while d < rows:
            keep = t >= d
            a_sh = jnp.where(keep, pltpu.roll(a, d, 0), 1.0)
            b_sh = jnp.where(keep, pltpu.roll(bt, d, 0), 0.0)
            bt = a * b_sh + bt
            a = a * a_sh
            d *= 2
        h = bt + a * h_ref[b]
        h_ref[b] = h[rows - 1:rows, :]
        o_ref[b] = h * jax.nn.gelu(gb)

    @pl.when(c == pl.num_programs(1) - 1)
    def _():
        hout_ref[...] = h_ref[...]


def _lru(pc, cb8, h0, cw, vec, wa_bd, wx_bd):
    bsz, length, _ = pc.shape
    nb = SEQ_PER_STEP
    rows = min(length, 256)
    grid = (bsz // nb, length // rows)
    full = lambda shape: pl.BlockSpec(shape, lambda i, c: tuple(0 for _ in shape))
    return pl.pallas_call(
        functools.partial(_lru_kernel, nb=nb, rows=rows),
        grid=grid,
        in_specs=[pl.BlockSpec((nb, rows, PC_W), lambda i, c: (i, c, 0)),
                  pl.BlockSpec((nb, 8, GW), lambda i, c: (i, 0, 0)),
                  pl.BlockSpec((nb, 1, GW), lambda i, c: (i, 0, 0)),
                  full((4, GW)), full((8, GW)), full((GW, GW)), full((GW, GW))],
        out_specs=[pl.BlockSpec((nb, rows, GW), lambda i, c: (i, c, 0)),
                   pl.BlockSpec((nb, 1, GW), lambda i, c: (i, 0, 0))],
        out_shape=[jax.ShapeDtypeStruct((bsz, length, GW), f32),
                   jax.ShapeDtypeStruct((bsz, 1, GW), f32)],
        scratch_shapes=[pltpu.VMEM((nb, rows + 8, GW), f32),
                        pltpu.VMEM((nb, 1, GW), f32)],
        compiler_params=_cparams(("arbitrary", "arbitrary")),
        name="lru",
    )(pc, cb8, h0, cw, vec, wa_bd, wx_bd)


def _ret_kernel(pd_ref, r0_ref, cos_ref, sin_ref, tab_ref, vec_ref, o_ref, rout_ref, s_ref, *, nb):
    c = pl.program_id(1)
    rows = nb * CHUNK
    m = _masks(nb)

    @pl.when(c == 0)
    def _():
        s_ref[...] = r0_ref[...]

    tile = lambda x: jnp.concatenate([x] * nb, axis=0)
    cos, sin = tile(cos_ref[...]), tile(sin_ref[...])
    dec, qgs, kgs = tile(tab_ref[0]), tile(tab_ref[1]), tile(tab_ref[2])
    vec = vec_ref[...]
    gl, norm_w, norm_b = vec[0:1, :], vec[1:2, :], vec[2:3, :]
    low_half = m["s"] < (HEAD_DIM // 2)

    def rot(x):
        swapped = jnp.where(low_half, pltpu.roll(x, GW - HEAD_DIM // 2, 1), pltpu.roll(x, HEAD_DIM // 2, 1))
        return x * cos + swapped * sin

    q = rot(pd_ref[:, :, 0:GW].reshape(rows, GW))
    k = rot(pd_ref[:, :, GW:2 * GW].reshape(rows, GW)) * (HEAD_DIM ** -0.5)
    v = pd_ref[:, :, 2 * GW:3 * GW].reshape(rows, GW)
    gate = pd_ref[:, :, 3 * GW:4 * GW].reshape(rows, GW)
    v_p = PREP(v)
    qk = _mm_bd(PREP(q), PREP(k), m, nb, NT) * dec
    qg_p, kg_p = PREP(q * qgs), PREP(k * kgs)
    s_old = [s_ref[b] for b in range(nb)]
    o = _each(nb, lambda b: _mm(_rsp(qg_p, b), PREP(s_old[b]))) + _mm_bd(PREP(qk), v_p, m, nb)
    for b in range(nb):
        s_ref[b] = s_old[b] * gl + jnp.where(m["bd"], _mm(_rsp(kg_p, b), _rsp(v_p, b), TN), 0.0)
    mean = _segsum(o, m) * (1.0 / HEAD_DIM)
    oc = o - mean
    var = _segsum(oc * oc, m) * (1.0 / HEAD_DIM)
    on = oc * lax.rsqrt(var + 1e-5) * norm_w + norm_b
    o_ref[...] = (gate * jax.nn.sigmoid(gate) * on).reshape(nb, CHUNK, GW)

    @pl.when(c == pl.num_programs(1) - 1)
    def _():
        rout_ref[...] = s_ref[...]


def _ret(pd, r0_bd, cos, sin, tab, vec):
    bsz, length, _ = pd.shape
    nb = SEQ_PER_STEP
    grid = (bsz // nb, length // CHUNK)
    full = lambda shape: pl.BlockSpec(shape, lambda i, c: tuple(0 for _ in shape))
    return pl.pallas_call(
        functools.partial(_ret_kernel, nb=nb),
        grid=grid,
        in_specs=[pl.BlockSpec((nb, CHUNK, PD_W), lambda i, c: (i, c, 0)),
                  pl.BlockSpec((nb, GW, GW), lambda i, c: (i, 0, 0)),
                  pl.BlockSpec((CHUNK, GW), lambda i, c: (c, 0)),
                  pl.BlockSpec((CHUNK, GW), lambda i, c: (c, 0)),
                  full((3, CHUNK, GW)), full((8, GW))],
        out_specs=[pl.BlockSpec((nb, CHUNK, GW), lambda i, c: (i, c, 0)),
                   pl.BlockSpec((nb, GW, GW), lambda i, c: (i, 0, 0))],
        out_shape=[jax.ShapeDtypeStruct((bsz, length, GW), f32),
                   jax.ShapeDtypeStruct((bsz, GW, GW), f32)],
        scratch_shapes=[pltpu.VMEM((nb, GW, GW), f32)],
        compiler_params=_cparams(("arbitrary", "arbitrary")),
        name="ret",
    )(pd, r0_bd, cos, sin, tab, vec)


def _post_kernel(x_ref, oa_ref, ob_ref, oc_ref, od_ref, wout_ref, nw_ref, wr_ref, br_ref, x1g_ref):
    acc = x_ref[...]
    for j, ref in enumerate((oa_ref, ob_ref, oc_ref, od_ref)):
        acc = acc + _dot(ref[...].astype(bf16), wout_ref[j * GW:(j + 1) * GW, :])
    x1g_ref[:, 0:D_MODEL] = acc
    tn = acc * lax.rsqrt(jnp.mean(acc * acc, axis=-1, keepdims=True) + 1e-6) * nw_ref[...]
    logits = _mm(_p3(tn), _p3(wr_ref[...])) + br_ref[...]
    lane = _iota(logits.shape, 1)
    lane_f = lane.astype(f32)
    ninf = -jnp.inf
    is_g = (lane >= N_EXPERTS) & (lane < N_EXPERTS + N_GROUPS)
    lg = jnp.where(is_g, logits, ninf)
    gmax = jnp.max(lg, axis=-1, keepdims=True)
    grp_f = jnp.min(jnp.where(lg == gmax, lane_f, 1e9), axis=-1, keepdims=True) - N_EXPERTS
    p_grp = 1.0 / jnp.sum(jnp.exp(lg - gmax), axis=-1, keepdims=True)
    in_grp = (lane >> 2) == grp_f.astype(jnp.int32)
    le = jnp.where(in_grp, logits, ninf)
    v1 = jnp.max(le, axis=-1, keepdims=True)
    i1 = jnp.min(jnp.where(le == v1, lane_f, 1e9), axis=-1, keepdims=True)
    le2 = jnp.where(lane_f == i1, ninf, le)
    v2 = jnp.max(le2, axis=-1, keepdims=True)
    i2 = jnp.min(jnp.where(le2 == v2, lane_f, 1e9), axis=-1, keepdims=True)
    e2 = jnp.exp(v2 - v1)
    wt1 = p_grp / (1.0 + e2)
    wt2 = p_grp * e2 / (1.0 + e2)
    gates = jnp.where(lane_f == i1, wt1, 0.0) + jnp.where(lane_f == i2, wt2, 0.0)
    x1g_ref[:, D_MODEL:] = jnp.where(lane == GRP_LANE, grp_f, gates)


def _post(x2d, outs, wout, nw, wr, br, tm):
    t = x2d.shape[0]
    row = lambda w: pl.BlockSpec((tm, w), lambda i: (i, 0))
    full = lambda shape: pl.BlockSpec(shape, lambda i: tuple(0 for _ in shape))
    return pl.pallas_call(
        _post_kernel,
        grid=(t // tm,),
        in_specs=[row(D_MODEL), row(GW), row(GW), row(GW), row(GW),
                  full((D_MODEL, D_MODEL)), full((1, D_MODEL)), full((D_MODEL, 128)), full((1, 128))],
        out_specs=row(XG_W),
        out_shape=jax.ShapeDtypeStruct((t, XG_W), f32),
        compiler_params=_cparams(("parallel",)),
        name="post",
    )(x2d, *outs, wout, nw, wr, br)


def _perm_kernel(pos_ref, src_ref, *rest, scatter, rb):
    dst_ref, sem = rest[-2], rest[-1]
    base = pl.program_id(0) * rb

    def row_copy(t, p):
        if scatter:
            return pltpu.make_async_copy(src_ref.at[pl.ds(t, 1)], dst_ref.at[pl.ds(p, 1)], sem)
        return pltpu.make_async_copy(src_ref.at[pl.ds(p, 1)], dst_ref.at[pl.ds(t, 1)], sem)

    def issue(r, carry):
        row_copy(base + r, pos_ref[0, 0, r]).start()
        return carry

    def drain(r, carry):
        row_copy(base + r, pos_ref[0, 0, r]).wait()
        return carry

    lax.fori_loop(0, rb, issue, 0, unroll=8)
    lax.fori_loop(0, rb, drain, 0, unroll=8)


def _permute_rows(pos, src, n_dst, scatter, rb):
    t = pos.shape[0]
    width = src.shape[1]
    pos3 = pos.reshape(t // rb, 1, rb)
    in_specs = [pl.BlockSpec((1, 1, rb), lambda i: (i, 0, 0), memory_space=pltpu.SMEM),
                pl.BlockSpec(memory_space=pl.ANY)]
    args = [pos3, src]
    aliases = {}
    if scatter:
        in_specs.append(pl.BlockSpec(memory_space=pl.ANY))
        args.append(jnp.zeros((n_dst, width), src.dtype))
        aliases = {2: 0}
    return pl.pallas_call(
        functools.partial(_perm_kernel, scatter=scatter, rb=rb),
        grid=(t // rb,),
        in_specs=in_specs,
        out_specs=pl.BlockSpec(memory_space=pl.ANY),
        out_shape=jax.ShapeDtypeStruct((n_dst, width), src.dtype),
        scratch_shapes=[pltpu.SemaphoreType.DMA],
        input_output_aliases=aliases,
        compiler_params=pltpu.CompilerParams(dimension_semantics=("arbitrary",)),
        name="sort_rows" if scatter else "unsort_rows",
    )(*args)


def _route(x1g, tm):
    t = x1g.shape[0]
    n_tiles = t // tm + N_GROUPS
    grp = x1g[:, D_MODEL + GRP_LANE].astype(jnp.int32)
    onehot = (grp[:, None] == jnp.arange(N_GROUPS, dtype=jnp.int32)[None, :]).astype(jnp.int32)
    csum = jnp.cumsum(onehot, axis=0)
    counts = csum[-1]
    padded = ((counts + tm - 1) // tm) * tm
    ends = jnp.cumsum(padded)
    pos = jnp.sum(onehot * (csum - 1 + (ends - padded)[None, :]), axis=1)
    tile_start = jnp.arange(n_tiles, dtype=jnp.int32) * tm
    tile_gid = jnp.minimum(jnp.sum((tile_start[:, None] >= ends[None, :]).astype(jnp.int32), axis=1),
                           N_GROUPS - 1)
    tile_valid = (tile_start < ends[-1]).astype(jnp.int32)
    return pos.astype(jnp.int32), tile_gid.astype(jnp.int32), tile_valid, n_tiles


def _moe_kernel(gid_ref, valid_ref, xs_ref, wg_ref, wu_ref, wd_ref, nffn_ref, nf_ref, o_ref, *, final_norm):
    i = pl.program_id(0)

    @pl.when(valid_ref[i] == 0)
    def _():
        o_ref[...] = jnp.zeros(o_ref.shape, o_ref.dtype)

    @pl.when(valid_ref[i] != 0)
    def _():
        x1 = xs_ref[:, 0:D_MODEL]
        gates = xs_ref[:, D_MODEL:]
        tb = (x1 * lax.rsqrt(jnp.mean(x1 * x1, axis=-1, keepdims=True) + 1e-6) * nffn_ref[...]).astype(bf16)
        lane = _iota(gates.shape, 1)
        first = gid_ref[i] * EXPERTS_PER_GROUP
        acc = x1
        for j in range(EXPERTS_PER_GROUP):
            hg = _dot(tb, wg_ref[j])
            hu = _dot(tb, wu_ref[j])
            he = (hg * jax.nn.sigmoid(hg) * hu).astype(bf16)
            ge = jnp.sum(jnp.where(lane == first + j, gates, 0.0), axis=-1, keepdims=True)
            acc = acc + ge * _dot(he, wd_ref[j])
        if final_norm:
            acc = acc * lax.rsqrt(jnp.mean(acc * acc, axis=-1, keepdims=True) + 1e-6) * nf_ref[...]
        o_ref[...] = acc


def _moe(xs, tile_gid, tile_valid, wg, wu, wd, nffn, nf, tm, final_norm):
    n_tiles = xs.shape[0] // tm
    epg = EXPERTS_PER_GROUP
    grid_spec = pltpu.PrefetchScalarGridSpec(
        num_scalar_prefetch=2,
        grid=(n_tiles,),
        in_specs=[pl.BlockSpec((tm, XG_W), lambda i, gid, valid: (i, 0)),
                  pl.BlockSpec((epg, D_MODEL, D_EXPERT), lambda i, gid, valid: (gid[i], 0, 0)),
                  pl.BlockSpec((epg, D_MODEL, D_EXPERT), lambda i, gid, valid: (gid[i], 0, 0)),
                  pl.BlockSpec((epg, D_EXPERT, D_MODEL), lambda i, gid, valid: (gid[i], 0, 0)),
                  pl.BlockSpec((1, D_MODEL), lambda i, gid, valid: (0, 0)),
                  pl.BlockSpec((1, D_MODEL), lambda i, gid, valid: (0, 0))],
        out_specs=pl.BlockSpec((tm, D_MODEL), lambda i, gid, valid: (i, 0)),
    )
    return pl.pallas_call(
        functools.partial(_moe_kernel, final_norm=final_norm),
        grid_spec=grid_spec,
        out_shape=jax.ShapeDtypeStruct((n_tiles * tm, D_MODEL), f32),
        compiler_params=_cparams(("arbitrary",)),
        name="moe",
    )(tile_gid, tile_valid, xs, wg, wu, wd, nffn, nf)


def _to_bd(s):
    bsz = s.shape[0]
    eye = jnp.eye(N_HEADS, dtype=s.dtype)
    return jnp.einsum("bhij,hg->bhigj", s, eye).reshape(bsz, GW, GW)


def _from_bd(s):
    bsz = s.shape[0]
    s5 = s.reshape(bsz, N_HEADS, HEAD_DIM, N_HEADS, HEAD_DIM)
    return jnp.stack([s5[:, h, :, h, :] for h in range(N_HEADS)], axis=1)


def _pad_hist(buf):
    return jnp.pad(buf, ((0, 0), (5, 0), (0, 0)))


def _rows8(*rows):
    out = [jnp.reshape(r, (1, -1)).astype(f32) for r in rows]
    width = out[0].shape[1]
    out += [jnp.zeros((1, width), f32)] * (8 - len(out))
    return jnp.concatenate(out, axis=0)


def _rep_head(x):
    return jnp.repeat(x, HEAD_DIM)


def _block_diag4(w):
    eye = jnp.eye(N_HEADS, dtype=w.dtype)
    return jnp.einsum("hij,hg->higj", w, eye).reshape(GW, GW)


def _rope_tables(pos):
    half = HEAD_DIM // 2
    inv = ROPE_BASE ** (-jnp.arange(half, dtype=f32) / half)
    ang = pos.astype(f32)[:, None] * inv
    cos, sin = jnp.cos(ang), jnp.sin(ang)
    cos_h = jnp.concatenate([cos, cos], axis=-1)
    sin_h = jnp.concatenate([-sin, sin], axis=-1)
    return jnp.tile(cos_h, (1, N_HEADS)), jnp.tile(sin_h, (1, N_HEADS))


def _ret_tables():
    log_gamma = jnp.log1p(-jnp.exp2(-5.0 - jnp.arange(N_HEADS, dtype=f32)))
    lg = _rep_head(log_gamma)[None, :]
    g = jnp.cumsum(jnp.broadcast_to(lg, (CHUNK, GW)), axis=0)
    g_h = g[:, ::HEAD_DIM]
    diff = g_h[:, None, :] - g_h[None, :, :]
    incl = jnp.tril(jnp.ones((CHUNK, CHUNK), dtype=bool))[:, :, None]
    dec = jnp.exp(jnp.where(incl, diff, -jnp.inf))
    dec = jnp.transpose(dec, (0, 2, 1)).reshape(CHUNK, GW)
    qgs = jnp.exp(g)
    kgs = jnp.exp(g[-1:] - g)
    gl = jnp.exp(g[-1:])
    return jnp.stack([dec, qgs, kgs]), gl


def _layer_params(l, p):
    f = lambda a: a[l]
    w_in = f(p["w_in"])
    c = [0, 768, 1024, 1028, 1032, 2056, 2312, 2568, 3336, 3592]
    a_qkv, a_z, a_b, a_a, b_rw, c_x, c_g, d_qkv, d_g = (w_in[:, c[i]:c[i + 1]] for i in range(9))
    rep = lambda w: jnp.repeat(w, HEAD_DIM, axis=1)
    w_all = jnp.concatenate([a_qkv, a_z, rep(a_b), rep(a_a), b_rw, c_x, c_g, d_qkv, d_g], axis=1).astype(bf16)
    lp = dict(
        norm_mix_w=f(p["norm_mix_w"])[None, :],
        w_all=w_all,
        gdn_cw=f(p["gdn_conv_w"]),
        gdn_vec=_rows8(-jnp.exp(_rep_head(f(p["gdn_a_log"]))), _rep_head(f(p["gdn_dt_bias"])),
                       jnp.tile(f(p["gdn_norm_w"]), N_HEADS)),
        rwkv_vec=_rows8(f(p["rwkv_w0"]), f(p["rwkv_a0"]), f(p["rwkv_k_k"]), f(p["rwkv_k_a"]),
                        f(p["rwkv_r_k"]).reshape(-1), f(p["rwkv_ln_w"]), f(p["rwkv_ln_b"])),
        rwkv_mu=f(p["rwkv_mu"])[None, :],
        rwkv_wup=f(p["rwkv_w_up"]), rwkv_aup=f(p["rwkv_a_up"]), rwkv_gup=f(p["rwkv_g_up"]),
        lru_cw=f(p["lru_conv_w"]),
        lru_vec=_rows8(f(p["lru_conv_b"]), f(p["lru_ba"]), f(p["lru_bx"]),
                       jax.nn.softplus(-f(p["lru_lambda"]))),
        lru_wa=_block_diag4(f(p["lru_wa"])).astype(bf16),
        lru_wx=_block_diag4(f(p["lru_wx"])).astype(bf16),
        ret_norm=(f(p["ret_norm_w"]), f(p["ret_norm_b"])),
        w_out=f(p["w_out"]).astype(bf16),
        norm_ffn_w=f(p["norm_ffn_w"])[None, :],
        w_router=jnp.concatenate([f(p["moe_router_e"]), f(p["moe_router_g"]),
                                  jnp.zeros((D_MODEL, 128 - N_EXPERTS - N_GROUPS), f32)], axis=1),
        b_router=jnp.concatenate([f(p["moe_router_e_b"]), f(p["moe_router_g_b"]),
                                  jnp.zeros((128 - N_EXPERTS - N_GROUPS,), f32)])[None, :],
        moe_wg=f(p["moe_w_gate"]).astype(bf16),
        moe_wu=f(p["moe_w_up"]).astype(bf16),
        moe_wd=f(p["moe_w_down"]).astype(bf16),
    )
    return lp


def _trunk(x, states, pos, layer_ps, norm_final_w):
    bsz, length, _ = x.shape
    t = bsz * length
    tm = min(512, t)
    assert t % tm == 0 and bsz % SEQ_PER_STEP == 0 and length % CHUNK == 0
    cos, sin = _rope_tables(pos)
    ret_tab, ret_gl = _ret_tables()
    new = {k: [] for k in ("gdn", "gdn_conv", "rwkv", "rwkv_shift", "lru", "lru_conv", "ret")}
    x2d = x.reshape(t, D_MODEL)
    for l, lp in enumerate(layer_ps):
        pa, pb, pc, pd = _proj(x2d, lp["norm_mix_w"], lp["w_all"], tm)
        pa = pa.reshape(bsz, length, PA_W)
        pb = pb.reshape(bsz, length, PB_W)
        pc = pc.reshape(bsz, length, PC_W)
        pd = pd.reshape(bsz, length, PD_W)
        o_a, s_gdn = _gdn(pa, _pad_hist(states["gdn_conv"][l]), _to_bd(states["gdn"][l]),
                          lp["gdn_cw"], lp["gdn_vec"])
        o_b, s_rwkv = _rwkv(pb, states["rwkv_shift"][l], _to_bd(states["rwkv"][l]), lp["rwkv_vec"],
                            lp["rwkv_mu"], lp["rwkv_wup"], lp["rwkv_aup"], lp["rwkv_gup"])
        o_c, h_lru = _lru(pc, _pad_hist(states["lru_conv"][l]), states["lru"][l][:, None, :],
                          lp["lru_cw"], lp["lru_vec"], lp["lru_wa"], lp["lru_wx"])
        o_d, s_ret = _ret(pd, _to_bd(states["ret"][l]), cos, sin, ret_tab,
                          _rows8(ret_gl, lp["ret_norm"][0], lp["ret_norm"][1]))
        outs = [o.reshape(t, GW) for o in (o_a, o_b, o_c, o_d)]
        x1g = _post(x2d, outs, lp["w_out"], lp["norm_ffn_w"], lp["w_router"], lp["b_router"], tm)
        pos_sorted, tile_gid, tile_valid, n_tiles = _route(x1g, tm)
        xs = _permute_rows(pos_sorted, x1g, n_tiles * tm, True, tm)
        ys = _moe(xs, tile_gid, tile_valid, lp["moe_wg"], lp["moe_wu"], lp["moe_wd"], lp["norm_ffn_w"],
                  norm_final_w[None, :], tm, final_norm=(l == len(layer_ps) - 1))
        x2d = _permute_rows(pos_sorted, ys, t, False, tm)
        new["gdn"].append(_from_bd(s_gdn))
        new["gdn_conv"].append(pa[:, length - 3:, 0:3 * GW])
        new["rwkv"].append(_from_bd(s_rwkv))
        new["rwkv_shift"].append(pb[:, length - 1:, :])
        new["lru"].append(h_lru[:, 0, :])
        new["lru_conv"].append(pc[:, length - 3:, 0:GW])
        new["ret"].append(_from_bd(s_ret))
    return x2d.reshape(bsz, length, D_MODEL), {k: jnp.stack(v) for k, v in new.items()}


def _zero_states(bsz, dtype):
    return {"gdn": jnp.zeros((DEPTH, bsz, N_HEADS, HEAD_DIM, HEAD_DIM), dtype),
            "gdn_conv": jnp.zeros((DEPTH, bsz, 3, 3 * GW), dtype),
            "rwkv": jnp.zeros((DEPTH, bsz, N_HEADS, HEAD_DIM, HEAD_DIM), dtype),
            "rwkv_shift": jnp.zeros((DEPTH, bsz, 1, PB_W), dtype),
            "lru": jnp.zeros((DEPTH, bsz, GW), dtype),
            "lru_conv": jnp.zeros((DEPTH, bsz, 3, GW), dtype),
            "ret": jnp.zeros((DEPTH, bsz, N_HEADS, HEAD_DIM, HEAD_DIM), dtype)}


def kernel(x_prompt, x_sample, state_gdn, state_gdn_conv, state_rwkv, state_rwkv_shift, state_lru, state_lru_conv, state_ret, norm_mix_w, w_in, gdn_conv_w, gdn_a_log, gdn_dt_bias, gdn_norm_w, rwkv_mu, rwkv_w0, rwkv_w_up, rwkv_a0, rwkv_a_up, rwkv_g_up, rwkv_k_k, rwkv_k_a, rwkv_r_k, rwkv_ln_w, rwkv_ln_b, lru_conv_w, lru_conv_b, lru_wa, lru_ba, lru_wx, lru_bx, lru_lambda, ret_norm_w, ret_norm_b, w_out, norm_ffn_w, moe_router_g, moe_router_g_b, moe_router_e, moe_router_e_b, moe_w_gate, moe_w_up, moe_w_down, norm_final_w):
    p = dict(norm_mix_w=norm_mix_w, w_in=w_in, gdn_conv_w=gdn_conv_w, gdn_a_log=gdn_a_log,
             gdn_dt_bias=gdn_dt_bias, gdn_norm_w=gdn_norm_w, rwkv_mu=rwkv_mu, rwkv_w0=rwkv_w0,
             rwkv_w_up=rwkv_w_up, rwkv_a0=rwkv_a0, rwkv_a_up=rwkv_a_up, rwkv_g_up=rwkv_g_up,
             rwkv_k_k=rwkv_k_k, rwkv_k_a=rwkv_k_a, rwkv_r_k=rwkv_r_k, rwkv_ln_w=rwkv_ln_w,
             rwkv_ln_b=rwkv_ln_b, lru_conv_w=lru_conv_w, lru_conv_b=lru_conv_b, lru_wa=lru_wa,
             lru_ba=lru_ba, lru_wx=lru_wx, lru_bx=lru_bx, lru_lambda=lru_lambda,
             ret_norm_w=ret_norm_w, ret_norm_b=ret_norm_b, w_out=w_out, norm_ffn_w=norm_ffn_w,
             moe_router_g=moe_router_g, moe_router_g_b=moe_router_g_b, moe_router_e=moe_router_e,
             moe_router_e_b=moe_router_e_b, moe_w_gate=moe_w_gate, moe_w_up=moe_w_up,
             moe_w_down=moe_w_down)
    depth = w_in.shape[0]
    layer_ps = [_layer_params(l, p) for l in range(depth)]
    pos_p = jnp.arange(x_prompt.shape[1], dtype=jnp.int32)
    y_p, new_p = _trunk(x_prompt, _zero_states(x_prompt.shape[0], x_prompt.dtype), pos_p, layer_ps, norm_final_w)
    states_s = {"gdn": state_gdn, "gdn_conv": state_gdn_conv, "rwkv": state_rwkv,
                "rwkv_shift": state_rwkv_shift, "lru": state_lru, "lru_conv": state_lru_conv,
                "ret": state_ret}
    pos_s = PAST_LEN + jnp.arange(x_sample.shape[1], dtype=jnp.int32)
    y_s, new_s = _trunk(x_sample, states_s, pos_s, layer_ps, norm_final_w)
    return (y_p, y_s,
            new_p["gdn"], new_s["gdn"], new_p["gdn_conv"], new_s["gdn_conv"],
            new_p["rwkv"], new_s["rwkv"], new_p["rwkv_shift"], new_s["rwkv_shift"],
            new_p["lru"], new_s["lru"], new_p["lru_conv"], new_s["lru_conv"],
            new_p["ret"], new_s["ret"])
```

```python
import functools

import jax
import jax.numpy as jnp
from jax import lax
from jax.experimental import pallas as pl
from jax.experimental.pallas import tpu as pltpu

f32 = jnp.float32
bf16 = jnp.bfloat16
HIGHEST = lax.Precision.HIGHEST

D_MODEL = 1024
N_HEADS = 4
HEAD_DIM = 64
GW = N_HEADS * HEAD_DIM
CHUNK = 64
DEPTH = 2
PAST_LEN = 4096
RWKV_W_SCALE = 0.606531
RWKV_LN_EPS = 64e-5
LRU_C = 8.0
ROPE_BASE = 10000.0
N_GROUPS = 4
EXPERTS_PER_GROUP = 4
N_EXPERTS = 16
D_EXPERT = 512

PA_W = 3 * GW + GW + 2 * GW
PB_W = 1024
PC_W = 2 * GW
PD_W = 3 * GW + GW
P_ALL = PA_W + PB_W + PC_W + PD_W

XG_W = D_MODEL + 128
GRP_LANE = 16

VMEM_LIMIT_BYTES = 56 * 1024 * 1024
SEQ_PER_STEP = 4

NN = (((1,), (0,)), ((), ()))
NT = (((1,), (1,)), ((), ()))
TN = (((0,), (0,)), ((), ()))


def _dot(a, b, dims=NN, precision=None):
    return lax.dot_general(a, b, dims, precision=precision, preferred_element_type=f32)


def _p1(x):
    return (x.astype(bf16),)


def _p3(x):
    hi = x.astype(bf16)
    return (hi, (x - hi.astype(f32)).astype(bf16))


def _mm(ap, bp, dims=NN):
    out = _dot(ap[0], bp[0], dims)
    if len(bp) > 1:
        out = out + _dot(ap[0], bp[1], dims)
    if len(ap) > 1:
        out = out + _dot(ap[1], bp[0], dims)
    return out


def _iota(shape, dim):
    return lax.broadcasted_iota(jnp.int32, shape, dim)


def _masks(nb):
    rows = nb * CHUNK
    t = _iota((rows, GW), 0) & (CHUNK - 1)
    lane = _iota((rows, GW), 1)
    s = lane & (HEAD_DIM - 1)
    r2 = _iota((GW, GW), 0) >> 6
    c2 = _iota((GW, GW), 1) >> 6
    bd = r2 == c2
    ts = _iota((CHUNK, CHUNK), 0)
    ss = _iota((CHUNK, CHUNK), 1)
    return dict(
        t=t, lane=lane, s=s, bd=bd,
        incl=t >= s, strict=t > s,
        blk16=(t >> 4) == (s >> 4), blk32=(t >> 5) == (s >> 5),
        bones=jnp.where(bd, 1.0, 0.0).astype(bf16),
        lt=jnp.where(ts >= ss, 1.0, 0.0).astype(bf16),
    )


def _rs(x, b):
    return x[b * CHUNK:(b + 1) * CHUNK]


def _rsp(parts, b):
    return tuple(_rs(p, b) for p in parts)


def _each(nb, fn):
    return jnp.concatenate([fn(b) for b in range(nb)], axis=0)


def _bd(parts, m):
    return tuple(
        jnp.where(m["bd"], jnp.concatenate([p] * N_HEADS, axis=0), jnp.zeros((), p.dtype))
        for p in parts)


def _mm_bd(lp, rp, m, nb, dims=NN):
    return _each(nb, lambda b: _mm(_rsp(lp, b), _bd(_rsp(rp, b), m), dims))


def _segsum(x, m):
    return _mm(_p3(x), (m["bones"],))


def _p_exact(x):
    p1 = x.astype(bf16)
    r1 = x - p1.astype(f32)
    p2 = r1.astype(bf16)
    return (p1, p2, (r1 - p2.astype(f32)).astype(bf16))


def _cumsum_t(x, m, nb):
    parts = _p_exact(x)

    def one(b):
        acc = _dot(m["lt"], _rs(parts[0], b))
        for p in parts[1:]:
            acc = acc + _dot(m["lt"], _rs(p, b))
        return acc

    return _each(nb, one)


def _last_row(x, nb):
    return _each(nb, lambda b: jnp.broadcast_to(_rs(x, b)[CHUNK - 1:CHUNK], (CHUNK, x.shape[1])))


def _softplus(x):
    return jnp.maximum(x, 0.0) + jnp.log1p(jnp.exp(-jnp.abs(x)))


def _tri_inv_q(a, m, nb, prep):
    mm = lambda x, y: _mm_bd(prep(x), prep(y), m, nb)
    m1 = -jnp.where(m["blk16"], a, 0.0)
    m2 = mm(m1, m1)
    m4 = mm(m2, m2)
    m8 = mm(m4, m4)
    q = m1
    q = q + m2 + mm(q, m2)
    q = q + m4 + mm(q, m4)
    q = q + m8 + mm(q, m8)
    for e in (jnp.where(m["blk32"] & jnp.logical_not(m["blk16"]), a, 0.0),
              jnp.where(m["blk32"], 0.0, a)):
        x = e + mm(q, e)
        y = x + mm(x, q)
        q = q - y
    return q


PREP_INV = _p1
PREP = _p1


def _cparams(sem):
    return pltpu.CompilerParams(dimension_semantics=sem, vmem_limit_bytes=VMEM_LIMIT_BYTES)


def _proj_kernel(x_ref, nw_ref, w_ref, pa_ref, pb_ref, pc_ref, pd_ref):
    x = x_ref[...]
    h = (x * lax.rsqrt(jnp.mean(x * x, axis=-1, keepdims=True) + 1e-6) * nw_ref[...]).astype(bf16)
    off = 0
    for ref, width in ((pa_ref, PA_W), (pb_ref, PB_W), (pc_ref, PC_W), (pd_ref, PD_W)):
        ref[...] = _dot(h, w_ref[:, off:off + width])
        off += width


def _proj(x2d, t, nw, w_all, tm):
    widths = (PA_W, PB_W, PC_W, PD_W)
    return pl.pallas_call(
        _proj_kernel,
        grid=(t // tm,),
        in_specs=[pl.BlockSpec((tm, D_MODEL), lambda i: (i, 0)),
                  pl.BlockSpec((1, D_MODEL), lambda i: (0, 0)),
                  pl.BlockSpec((D_MODEL, P_ALL), lambda i: (0, 0))],
        out_specs=[pl.BlockSpec((tm, w), lambda i: (i, 0)) for w in widths],
        out_shape=[jax.ShapeDtypeStruct((t, w), f32) for w in widths],
        compiler_params=_cparams(("parallel",)),
        name="proj",
    )(x2d, nw, w_all)


def _conv4(xb_ref, rows, cw):
    y = xb_ref[:, 5:5 + rows, :] * cw[0:1, :]
    for j in range(1, 4):
        y = y + xb_ref[:, 5 + j:5 + j + rows, :] * cw[j:j + 1, :]
    return y


def _gdn_kernel(pa_ref, cb_ref, s0_ref, cw_ref, vec_ref, o_ref, sout_ref, xb_ref, s_ref, *, nb):
    c = pl.program_id(1)
    rows = nb * CHUNK
    m = _masks(nb)

    @pl.when(c == 0)
    def _():
        s_ref[...] = s0_ref[...]
        xb_ref[:, 0:8, :] = cb_ref[...]

    vec = vec_ref[...]
    neg_exp_alog, dt_bias, norm_w = vec[0:1, :], vec[1:2, :], vec[2:3, :]

    xb_ref[:, 8:8 + CHUNK, :] = pa_ref[:, :, 0:3 * GW]
    y = _conv4(xb_ref, CHUNK, cw_ref[...])
    xb_ref[:, 0:8, :] = xb_ref[:, CHUNK:CHUNK + 8, :]
    y = y.reshape(rows, 3 * GW)
    cq = y * jax.nn.sigmoid(y)
    q, k, v = cq[:, 0:GW], cq[:, GW:2 * GW], cq[:, 2 * GW:3 * GW]
    z = pa_ref[:, :, 3 * GW:4 * GW].reshape(rows, GW)
    b_raw = pa_ref[:, :, 4 * GW:5 * GW].reshape(rows, GW)
    a_raw = pa_ref[:, :, 5 * GW:6 * GW].reshape(rows, GW)
    q = q * lax.rsqrt(_segsum(q * q, m) + 1e-6) * (HEAD_DIM ** -0.5)
    k = k * lax.rsqrt(_segsum(k * k, m) + 1e-6)
    beta = jax.nn.sigmoid(b_raw)
    g = neg_exp_alog * _softplus(a_raw + dt_bias)
    cum = _cumsum_t(jnp.concatenate([g, jnp.where(m["strict"], g, 0.0)], axis=1), m, nb)
    gc = cum[:, 0:GW]
    decay = jnp.exp(jnp.where(m["incl"], cum[:, GW:2 * GW], -jnp.inf))
    eg = jnp.exp(gc)
    glast = _last_row(gc, nb)
    kb = k * beta
    k_p = PREP(k)
    a = jnp.where(m["strict"], _mm_bd(PREP(kb), k_p, m, nb, NT) * decay, 0.0)
    qt_p = PREP_INV(_tri_inv_q(a, m, nb, PREP_INV))
    rhs_w = kb * eg
    rhs_u = v * beta
    w = rhs_w + _mm_bd(qt_p, PREP_INV(rhs_w), m, nb)
    u = rhs_u + _mm_bd(qt_p, PREP_INV(rhs_u), m, nb)
    qk = _mm_bd(PREP(q), k_p, m, nb, NT) * decay
    qg_p = PREP(q * eg)
    kg_p = PREP(k * jnp.exp(glast - gc))
    w_p = PREP(w)
    s_old = [s_ref[b] for b in range(nb)]
    s_p = [PREP(s) for s in s_old]
    vn = u - _each(nb, lambda b: _mm(_rsp(w_p, b), s_p[b]))
    vn_p = PREP(vn)
    o = _each(nb, lambda b: _mm(_rsp(qg_p, b), s_p[b])) + _mm_bd(PREP(qk), vn_p, m, nb)
    for b in range(nb):
        upd = _mm(_rsp(kg_p, b), _rsp(vn_p, b), TN)
        s_ref[b] = s_old[b] * jnp.exp(_rs(gc, b)[CHUNK - 1:CHUNK]) + jnp.where(m["bd"], upd, 0.0)
    o = o * lax.rsqrt(_segsum(o * o, m) * (1.0 / HEAD_DIM) + 1e-6) * norm_w * (z * jax.nn.sigmoid(z))
    o_ref[...] = o.reshape(nb, CHUNK, GW)

    @pl.when(c == pl.num_programs(1) - 1)
    def _():
        sout_ref[...] = s_ref[...]


def _gdn(pa, cb8, s0_bd, cw, vec):
    bsz, length, _ = pa.shape
    nb = SEQ_PER_STEP
    grid = (bsz // nb, length // CHUNK)
    return pl.pallas_call(
        functools.partial(_gdn_kernel, nb=nb),
        grid=grid,
        in_specs=[pl.BlockSpec((nb, CHUNK, PA_W), lambda i, c: (i, c, 0)),
                  pl.BlockSpec((nb, 8, 3 * GW), lambda i, c: (i, 0, 0)),
                  pl.BlockSpec((nb, GW, GW), lambda i, c: (i, 0, 0)),
                  pl.BlockSpec((4, 3 * GW), lambda i, c: (0, 0)),
                  pl.BlockSpec((8, GW), lambda i, c: (0, 0))],
        out_specs=[pl.BlockSpec((nb, CHUNK, GW), lambda i, c: (i, c, 0)),
                   pl.BlockSpec((nb, GW, GW), lambda i, c: (i, 0, 0))],
        out_shape=[jax.ShapeDtypeStruct((bsz, length, GW), f32),
                   jax.ShapeDtypeStruct((bsz, GW, GW), f32)],
        scratch_shapes=[pltpu.VMEM((nb, CHUNK + 8, 3 * GW), f32),
                        pltpu.VMEM((nb, GW, GW), f32)],
        compiler_params=_cparams(("arbitrary", "arbitrary")),
        name="gdn",
    )(pa, cb8, s0_bd, cw, vec)


def _rwkv_kernel(pb_ref, sh_ref, s0_ref, vec_ref, mu_ref, wup_ref, aup_ref, gup_ref,
                 o_ref, sout_ref, prev_ref, s_ref, *, nb):
    c = pl.program_id(1)
    rows = nb * CHUNK
    m = _masks(nb)

    @pl.when(c == 0)
    def _():
        s_ref[...] = s0_ref[...]
        prev_ref[...] = sh_ref[...]

    vec = vec_ref[...]
    w0, a0, k_k, k_a, r_k, ln_w, ln_b = (vec[i:i + 1, :] for i in range(7))

    p = pb_ref[...].reshape(rows, PB_W)
    first = (_iota((rows, PB_W), 0) & (CHUNK - 1)) == 0
    carried = _each(nb, lambda b: jnp.broadcast_to(prev_ref[b], (CHUNK, PB_W)))
    prev = jnp.where(first, carried, pltpu.roll(p, 1, 0))
    for b in range(nb):
        prev_ref[b] = _rs(p, b)[CHUNK - 1:CHUNK, :]
    xs = p + (prev - p) * mu_ref[...]
    r, k, v = xs[:, 0:GW], xs[:, GW:2 * GW], xs[:, 2 * GW:3 * GW]
    wd, ad, gd = xs[:, 768:832], xs[:, 832:896], xs[:, 896:1024]
    logw = -RWKV_W_SCALE * jax.nn.sigmoid(w0 + _mm(_p3(jnp.tanh(wd)), _p3(wup_ref[...])))
    a = jax.nn.sigmoid(a0 + _mm(_p3(ad), _p3(aup_ref[...])))
    gate = _mm(_p3(jax.nn.sigmoid(gd)), _p3(gup_ref[...]))
    kk = k * k_k
    kk = kk * lax.rsqrt(_segsum(kk * kk, m) + 1e-6)
    k = k * (1.0 + (a - 1.0) * k_a)

    gc = _cumsum_t(logw, m, nb)
    glast = _last_row(gc, nb)
    eng = jnp.exp(-gc)
    kkd = kk * jnp.exp(gc - logw)
    bvec = a * kk
    rd = r * jnp.exp(gc)
    kkd_p, rd_p = PREP(kkd), PREP(rd)
    binv_p, kinv_p, v_p = PREP(bvec * eng), PREP(k * eng), PREP(v)

    def intra(b):
        lhs = tuple(jnp.concatenate([_rs(x, b), _rs(y, b)], axis=0) for x, y in zip(kkd_p, rd_p))
        return jnp.concatenate([_mm(lhs, _bd(_rsp(binv_p, b), m), NT),
                                _mm(lhs, _bd(_rsp(kinv_p, b), m), NT)], axis=1)

    prods = [intra(b) for b in range(nb)]
    top = jnp.concatenate([x[0:CHUNK] for x in prods], axis=0)
    bot = jnp.concatenate([x[CHUNK:] for x in prods], axis=0)
    amat = jnp.where(m["strict"], top[:, 0:GW], 0.0)
    bmat = jnp.where(m["strict"], top[:, GW:], 0.0)
    rb = jnp.where(m["incl"], bot[:, 0:GW], 0.0)
    rk = jnp.where(m["incl"], bot[:, GW:], 0.0)
    qt_p = PREP_INV(_tri_inv_q(amat, m, nb, PREP_INV))

    s_old = [s_ref[b] for b in range(nb)]
    s_p = [PREP(s) for s in s_old]
    rhs = _each(nb, lambda b: _mm(_rsp(kkd_p, b), s_p[b], NT)) + _mm_bd(PREP(bmat), v_p, m, nb)
    u = -(rhs + _mm_bd(qt_p, PREP_INV(rhs), m, nb))
    u_p = PREP(u)
    o = (_each(nb, lambda b: _mm(_rsp(rd_p, b), s_p[b], NT))
         + _mm_bd(PREP(rb), u_p, m, nb) + _mm_bd(PREP(rk), v_p, m, nb))
    dec_end = jnp.exp(glast - gc)
    bend_p, kend_p = PREP(bvec * dec_end), PREP(k * dec_end)
    for b in range(nb):
        lhs = tuple(jnp.concatenate([_rs(x, b), _rs(y, b)], axis=0) for x, y in zip(u_p, v_p))
        rhs2 = tuple(jnp.concatenate([_rs(x, b), _rs(y, b)], axis=0) for x, y in zip(bend_p, kend_p))
        upd = _mm(lhs, rhs2, TN)
        s_ref[b] = s_old[b] * jnp.exp(_rs(gc, b)[CHUNK - 1:CHUNK]) + jnp.where(m["bd"], upd, 0.0)

    mean = _segsum(o, m) * (1.0 / HEAD_DIM)
    oc = o - mean
    var = _segsum(oc * oc, m) * (1.0 / HEAD_DIM)
    on = oc * lax.rsqrt(var + RWKV_LN_EPS) * ln_w + ln_b
    bonus = _segsum(r * k * r_k, m) * v
    o_ref[...] = ((on + bonus) * gate).reshape(nb, CHUNK, GW)

    @pl.when(c == pl.num_programs(1) - 1)
    def _():
        sout_ref[...] = s_ref[...]


def _rwkv(pb, shift, s0_bd, vec, mu, wup, aup, gup):
    bsz, length, _ = pb.shape
    nb = SEQ_PER_STEP
    grid = (bsz // nb, length // CHUNK)
    full = lambda shape: pl.BlockSpec(shape, lambda i, c: tuple(0 for _ in shape))
    return pl.pallas_call(
        functools.partial(_rwkv_kernel, nb=nb),
        grid=grid,
        in_specs=[pl.BlockSpec((nb, CHUNK, PB_W), lambda i, c: (i, c, 0)),
                  pl.BlockSpec((nb, 1, PB_W), lambda i, c: (i, 0, 0)),
                  pl.BlockSpec((nb, GW, GW), lambda i, c: (i, 0, 0)),
                  full((8, GW)), full((1, PB_W)), full((64, GW)), full((64, GW)), full((128, GW))],
        out_specs=[pl.BlockSpec((nb, CHUNK, GW), lambda i, c: (i, c, 0)),
                   pl.BlockSpec((nb, GW, GW), lambda i, c: (i, 0, 0))],
        out_shape=[jax.ShapeDtypeStruct((bsz, length, GW), f32),
                   jax.ShapeDtypeStruct((bsz, GW, GW), f32)],
        scratch_shapes=[pltpu.VMEM((nb, 1, PB_W), f32),
                        pltpu.VMEM((nb, GW, GW), f32)],
        compiler_params=_cparams(("arbitrary", "arbitrary")),
        name="rwkv",
    )(pb, shift, s0_bd, vec, mu, wup, aup, gup)


def _neg_expm1(x):
    u = jnp.exp(x)
    um1 = u - 1.0
    lu = jnp.log(u)
    safe = jnp.where(um1 == 0.0, x, um1 * x / jnp.where(lu == 0.0, 1.0, lu))
    return -jnp.where(x < -0.5, um1, safe)


def _lru_kernel(pc_ref, cb_ref, h0_ref, cw_ref, vec_ref, wa_ref, wx_ref,
                o_ref, hout_ref, xb_ref, h_ref, *, nb, rows):
    c = pl.program_id(1)

    @pl.when(c == 0)
    def _():
        h_ref[...] = h0_ref[...]
        xb_ref[:, 0:8, :] = cb_ref[...]

    cw = cw_ref[...]
    vec = vec_ref[...]
    conv_b, ba, bx, sp_neg_lam = (vec[i:i + 1, :] for i in range(4))
    t = _iota((rows, GW), 0)

    xb_ref[:, 8:8 + rows, :] = pc_ref[:, :, 0:GW]
    xc_all = _conv4(xb_ref, rows, cw) + conv_b
    xb_ref[:, 0:8, :] = xb_ref[:, rows:rows + 8, :]
    for b in range(nb):
        xc = xc_all[b]
        gb = pc_ref[b, :, GW:2 * GW]
        xcp = PREP(xc)
        r = jax.nn.sigmoid(_mm(xcp, (wa_ref[...],)) + ba)
        i = jax.nn.sigmoid(_mm(xcp, (wx_ref[...],)) + bx)
        log_a = -LRU_C * r * sp_neg_lam
        a = jnp.exp(log_a)
        bt = jnp.sqrt(_neg_expm1(2.0 * log_a)) * (i * xc)
        d = 1
        while d < rows:
            keep = t >= d
            a_sh = jnp.where(keep, pltpu.roll(a, d, 0), 1.0)
            b_sh = jnp.where(keep, pltpu.roll(bt, d, 0), 0.0)
            bt = a * b_sh + bt
            a = a * a_sh
            d *= 2
        h = bt + a * h_ref[b]
        h_ref[b] = h[rows - 1:rows, :]
        o_ref[b] = h * jax.nn.gelu(gb)

    @pl.when(c == pl.num_programs(1) - 1)
    def _():
        hout_ref[...] = h_ref[...]


def _lru(pc, cb8, h0, cw, vec, wa_bd, wx_bd):
    bsz, length, _ = pc.shape
    nb = SEQ_PER_STEP
    rows = min(length, 256)
    grid = (bsz // nb, length // rows)
    full = lambda shape: pl.BlockSpec(shape, lambda i, c: tuple(0 for _ in shape))
    return pl.pallas_call(
        functools.partial(_lru_kernel, nb=nb, rows=rows),
        grid=grid,
        in_specs=[pl.BlockSpec((nb, rows, PC_W), lambda i, c: (i, c, 0)),
                  pl.BlockSpec((nb, 8, GW), lambda i, c: (i, 0, 0)),
                  pl.BlockSpec((nb, 1, GW), lambda i, c: (i, 0, 0)),
                  full((4, GW)), full((8, GW)), full((GW, GW)), full((GW, GW))],
        out_specs=[pl.BlockSpec((nb, rows, GW), lambda i, c: (i, c, 0)),
                   pl.BlockSpec((nb, 1, GW), lambda i, c: (i, 0, 0))],
        out_shape=[jax.ShapeDtypeStruct((bsz, length, GW), f32),
                   jax.ShapeDtypeStruct((bsz, 1, GW), f32)],
        scratch_shapes=[pltpu.VMEM((nb, rows + 8, GW), f32),
                        pltpu.VMEM((nb, 1, GW), f32)],
        compiler_params=_cparams(("arbitrary", "arbitrary")),
        name="lru",
    )(pc, cb8, h0, cw, vec, wa_bd, wx_bd)


def _ret_kernel(pd_ref, r0_ref, cos_ref, sin_ref, tab_ref, vec_ref, o_ref, rout_ref, s_ref, *, nb):
    c = pl.program_id(1)
    rows = nb * CHUNK
    m = _masks(nb)

    @pl.when(c == 0)
    def _():
        s_ref[...] = r0_ref[...]

    tile = lambda x: jnp.concatenate([x] * nb, axis=0)
    cos, sin = tile(cos_ref[...]), tile(sin_ref[...])
    dec, qgs, kgs = tile(tab_ref[0]), tile(tab_ref[1]), tile(tab_ref[2])
    vec = vec_ref[...]
    gl, norm_w, norm_b = vec[0:1, :], vec[1:2, :], vec[2:3, :]
    low_half = m["s"] < (HEAD_DIM // 2)

    def rot(x):
        swapped = jnp.where(low_half, pltpu.roll(x, GW - HEAD_DIM // 2, 1), pltpu.roll(x, HEAD_DIM // 2, 1))
        return x * cos + swapped * sin

    q = rot(pd_ref[:, :, 0:GW].reshape(rows, GW))
    k = rot(pd_ref[:, :, GW:2 * GW].reshape(rows, GW)) * (HEAD_DIM ** -0.5)
    v = pd_ref[:, :, 2 * GW:3 * GW].reshape(rows, GW)
    gate = pd_ref[:, :, 3 * GW:4 * GW].reshape(rows, GW)
    v_p = PREP(v)
    qk = _mm_bd(PREP(q), PREP(k), m, nb, NT) * dec
    qg_p, kg_p = PREP(q * qgs), PREP(k * kgs)
    s_old = [s_ref[b] for b in range(nb)]
    o = _each(nb, lambda b: _mm(_rsp(qg_p, b), PREP(s_old[b]))) + _mm_bd(PREP(qk), v_p, m, nb)
    for b in range(nb):
        s_ref[b] = s_old[b] * gl + jnp.where(m["bd"], _mm(_rsp(kg_p, b), _rsp(v_p, b), TN), 0.0)
    mean = _segsum(o, m) * (1.0 / HEAD_DIM)
    oc = o - mean
    var = _segsum(oc * oc, m) * (1.0 / HEAD_DIM)
    on = oc * lax.rsqrt(var + 1e-5) * norm_w + norm_b
    o_ref[...] = (gate * jax.nn.sigmoid(gate) * on).reshape(nb, CHUNK, GW)

    @pl.when(c == pl.num_programs(1) - 1)
    def _():
        rout_ref[...] = s_ref[...]


def _ret(pd, r0_bd, cos, sin, tab, vec):
    bsz, length, _ = pd.shape
    nb = SEQ_PER_STEP
    grid = (bsz // nb, length // CHUNK)
    full = lambda shape: pl.BlockSpec(shape, lambda i, c: tuple(0 for _ in shape))
    return pl.pallas_call(
        functools.partial(_ret_kernel, nb=nb),
        grid=grid,
        in_specs=[pl.BlockSpec((nb, CHUNK, PD_W), lambda i, c: (i, c, 0)),
                  pl.BlockSpec((nb, GW, GW), lambda i, c: (i, 0, 0)),
                  pl.BlockSpec((CHUNK, GW), lambda i, c: (c, 0)),
                  pl.BlockSpec((CHUNK, GW), lambda i, c: (c, 0)),
                  full((3, CHUNK, GW)), full((8, GW))],
        out_specs=[pl.BlockSpec((nb, CHUNK, GW), lambda i, c: (i, c, 0)),
                   pl.BlockSpec((nb, GW, GW), lambda i, c: (i, 0, 0))],
        out_shape=[jax.ShapeDtypeStruct((bsz, length, GW), f32),
                   jax.ShapeDtypeStruct((bsz, GW, GW), f32)],
        scratch_shapes=[pltpu.VMEM((nb, GW, GW), f32)],
        compiler_params=_cparams(("arbitrary", "arbitrary")),
        name="ret",
    )(pd, r0_bd, cos, sin, tab, vec)


def _post_kernel(x_ref, oa_ref, ob_ref, oc_ref, od_ref, wout_ref, nw_ref, wr_ref, br_ref, x1g_ref):
    acc = x_ref[...]
    for j, ref in enumerate((oa_ref, ob_ref, oc_ref, od_ref)):
        acc = acc + _dot(ref[...].astype(bf16), wout_ref[j * GW:(j + 1) * GW, :])
    x1g_ref[:, 0:D_MODEL] = acc
    tn = acc * lax.rsqrt(jnp.mean(acc * acc, axis=-1, keepdims=True) + 1e-6) * nw_ref[...]
    logits = _mm(_p3(tn), _p3(wr_ref[...])) + br_ref[...]
    lane = _iota(logits.shape, 1)
    lane_f = lane.astype(f32)
    ninf = -jnp.inf
    is_g = (lane >= N_EXPERTS) & (lane < N_EXPERTS + N_GROUPS)
    lg = jnp.where(is_g, logits, ninf)
    gmax = jnp.max(lg, axis=-1, keepdims=True)
    grp_f = jnp.min(jnp.where(lg == gmax, lane_f, 1e9), axis=-1, keepdims=True) - N_EXPERTS
    p_grp = 1.0 / jnp.sum(jnp.exp(lg - gmax), axis=-1, keepdims=True)
    in_grp = (lane >> 2) == grp_f.astype(jnp.int32)
    le = jnp.where(in_grp, logits, ninf)
    v1 = jnp.max(le, axis=-1, keepdims=True)
    i1 = jnp.min(jnp.where(le == v1, lane_f, 1e9), axis=-1, keepdims=True)
    le2 = jnp.where(lane_f == i1, ninf, le)
    v2 = jnp.max(le2, axis=-1, keepdims=True)
    i2 = jnp.min(jnp.where(le2 == v2, lane_f, 1e9), axis=-1, keepdims=True)
    e2 = jnp.exp(v2 - v1)
    wt1 = p_grp / (1.0 + e2)
    wt2 = p_grp * e2 / (1.0 + e2)
    gates = jnp.where(lane_f == i1, wt1, 0.0) + jnp.where(lane_f == i2, wt2, 0.0)
    x1g_ref[:, D_MODEL:] = jnp.where(lane == GRP_LANE, grp_f, gates)


def _post(x2d, outs, wout, nw, wr, br, tm):
    t = outs[0].shape[0]
    row = lambda w: pl.BlockSpec((tm, w), lambda i: (i, 0))
    full = lambda shape: pl.BlockSpec(shape, lambda i: tuple(0 for _ in shape))
    return pl.pallas_call(
        _post_kernel,
        grid=(t // tm,),
        in_specs=[row(D_MODEL), row(GW), row(GW), row(GW), row(GW),
                  full((D_MODEL, D_MODEL)), full((1, D_MODEL)), full((D_MODEL, 128)), full((1, 128))],
        out_specs=row(XG_W),
        out_shape=jax.ShapeDtypeStruct((t, XG_W), f32),
        compiler_params=_cparams(("parallel",)),
        name="post",
    )(x2d, *outs, wout, nw, wr, br)


ROW_DMA_UNROLL = 8


def _route(x1g, tm):
    t = x1g.shape[0]
    n_tiles = t // tm + N_GROUPS
    grp = x1g[:, D_MODEL + GRP_LANE].astype(jnp.int32)
    onehot = (grp[:, None] == jnp.arange(N_GROUPS, dtype=jnp.int32)[None, :]).astype(jnp.int32)
    csum = jnp.cumsum(onehot, axis=0)
    counts = csum[-1]
    padded = ((counts + tm - 1) // tm) * tm
    ends = jnp.cumsum(padded)
    pos = jnp.sum(onehot * (csum - 1 + (ends - padded)[None, :]), axis=1)
    token = jnp.full((n_tiles * tm,), -1, jnp.int32).at[pos].set(
        jnp.arange(t, dtype=jnp.int32), unique_indices=True)
    spare = t + (jnp.arange(n_tiles * tm, dtype=jnp.int32) % tm)
    gather_idx = jnp.maximum(token, 0)
    scatter_idx = jnp.where(token >= 0, token, spare)
    tile_start = jnp.arange(n_tiles, dtype=jnp.int32) * tm
    tile_gid = jnp.minimum(jnp.sum((tile_start[:, None] >= ends[None, :]).astype(jnp.int32), axis=1),
                           N_GROUPS - 1)
    tile_valid = (tile_start < ends[-1]).astype(jnp.int32)
    shape3 = (n_tiles, 1, tm)
    return gather_idx.reshape(shape3), scatter_idx.reshape(shape3), tile_gid.astype(jnp.int32), tile_valid


def _moe_kernel(gid_ref, valid_ref, gidx_ref, gidx_next_ref, sidx_ref, x1g_ref, wg_ref, wu_ref, wd_ref,
                nffn_ref, nf_ref, x2_ref, xbuf, obuf, gsem, ssem, *, final_norm, tm):
    i = pl.program_id(0)
    n = pl.num_programs(0)
    slot = i % 2

    def gather_copy(idx_ref, g, j, s):
        return pltpu.make_async_copy(x1g_ref.at[pl.ds(idx_ref[0, 0, g * 8 + j], 1)],
                                     xbuf.at[s, g, pl.ds(j, 1)], gsem.at[s])

    def scatter_copy(g, j):
        return pltpu.make_async_copy(obuf.at[g, pl.ds(j, 1)],
                                     x2_ref.at[pl.ds(sidx_ref[0, 0, g * 8 + j], 1)], ssem)

    def each_row(fn):
        def body(g, carry):
            for j in range(8):
                fn(g, j)
            return carry
        lax.fori_loop(0, tm // 8, body, 0, unroll=ROW_DMA_UNROLL // 8)

    @pl.when(i == 0)
    def _():
        each_row(lambda g, j: gather_copy(gidx_ref, g, j, 0).start())

    @pl.when(i + 1 < n)
    def _():
        each_row(lambda g, j: gather_copy(gidx_next_ref, g, j, 1 - slot).start())

    each_row(lambda g, j: gather_copy(gidx_ref, g, j, slot).wait())

    @pl.when(i > 0)
    def _():
        each_row(lambda g, j: scatter_copy(g, j).wait())

    @pl.when(valid_ref[i] == 0)
    def _():
        obuf[...] = jnp.zeros(obuf.shape, obuf.dtype)

    @pl.when(valid_ref[i] != 0)
    def _():
        rows = xbuf[slot].reshape(tm, XG_W)
        x1 = rows[:, 0:D_MODEL]
        gates = rows[:, D_MODEL:]
        tb = (x1 * lax.rsqrt(jnp.mean(x1 * x1, axis=-1, keepdims=True) + 1e-6) * nffn_ref[...]).astype(bf16)
        lane = _iota(gates.shape, 1)
        first = gid_ref[i] * EXPERTS_PER_GROUP
        acc = x1
        for j in range(EXPERTS_PER_GROUP):
            hg = _dot(tb, wg_ref[j])
            hu = _dot(tb, wu_ref[j])
            he = (hg * jax.nn.sigmoid(hg) * hu).astype(bf16)
            ge = jnp.sum(jnp.where(lane == first + j, gates, 0.0), axis=-1, keepdims=True)
            acc = acc + ge * _dot(he, wd_ref[j])
        if final_norm:
            acc = acc * lax.rsqrt(jnp.mean(acc * acc, axis=-1, keepdims=True) + 1e-6) * nf_ref[...]
        obuf[...] = acc.reshape(obuf.shape)

    each_row(lambda g, j: scatter_copy(g, j).start())

    @pl.when(i == n - 1)
    def _():
        each_row(lambda g, j: scatter_copy(g, j).wait())


def _moe(x1g, gather_idx, scatter_idx, tile_gid, tile_valid, wg, wu, wd, nffn, nf, tm, final_norm):
    t = x1g.shape[0]
    n_tiles = gather_idx.shape[0]
    epg = EXPERTS_PER_GROUP
    smem_tile = lambda fn: pl.BlockSpec((1, 1, tm), fn, memory_space=pltpu.SMEM)
    wspec = lambda shape: pl.BlockSpec(shape, lambda i, gid, valid: (gid[i], 0, 0))
    vec = pl.BlockSpec((1, D_MODEL), lambda i, gid, valid: (0, 0))
    grid_spec = pltpu.PrefetchScalarGridSpec(
        num_scalar_prefetch=2,
        grid=(n_tiles,),
        in_specs=[smem_tile(lambda i, gid, valid: (i, 0, 0)),
                  smem_tile(lambda i, gid, valid: (jnp.minimum(i + 1, n_tiles - 1), 0, 0)),
                  smem_tile(lambda i, gid, valid: (i, 0, 0)),
                  pl.BlockSpec(memory_space=pl.ANY),
                  wspec((epg, D_MODEL, D_EXPERT)), wspec((epg, D_MODEL, D_EXPERT)),
                  wspec((epg, D_EXPERT, D_MODEL)), vec, vec],
        out_specs=pl.BlockSpec(memory_space=pl.ANY),
        scratch_shapes=[pltpu.VMEM((2, tm // 8, 8, XG_W), f32), pltpu.VMEM((tm // 8, 8, D_MODEL), f32),
                        pltpu.SemaphoreType.DMA((2,)), pltpu.SemaphoreType.DMA],
    )
    return pl.pallas_call(
        functools.partial(_moe_kernel, final_norm=final_norm, tm=tm),
        grid_spec=grid_spec,
        out_shape=jax.ShapeDtypeStruct((t + tm, D_MODEL), f32),
        compiler_params=_cparams(("arbitrary",)),
        name="moe",
    )(tile_gid, tile_valid, gather_idx, gather_idx, scatter_idx, x1g, wg, wu, wd, nffn, nf)


def _to_bd(s):
    bsz = s.shape[0]
    eye = jnp.eye(N_HEADS, dtype=s.dtype)
    return jnp.einsum("bhij,hg->bhigj", s, eye).reshape(bsz, GW, GW)


def _from_bd(s):
    bsz = s.shape[0]
    s5 = s.reshape(bsz, N_HEADS, HEAD_DIM, N_HEADS, HEAD_DIM)
    return jnp.stack([s5[:, h, :, h, :] for h in range(N_HEADS)], axis=1)


def _pad_hist(buf):
    return jnp.pad(buf, ((0, 0), (5, 0), (0, 0)))


def _rows8(*rows):
    out = [jnp.reshape(r, (1, -1)).astype(f32) for r in rows]
    width = out[0].shape[1]
    out += [jnp.zeros((1, width), f32)] * (8 - len(out))
    return jnp.concatenate(out, axis=0)


def _rep_head(x):
    return jnp.repeat(x, HEAD_DIM)


def _block_diag4(w):
    eye = jnp.eye(N_HEADS, dtype=w.dtype)
    return jnp.einsum("hij,hg->higj", w, eye).reshape(GW, GW)


def _rope_tables(pos):
    half = HEAD_DIM // 2
    inv = ROPE_BASE ** (-jnp.arange(half, dtype=f32) / half)
    ang = pos.astype(f32)[:, None] * inv
    cos, sin = jnp.cos(ang), jnp.sin(ang)
    cos_h = jnp.concatenate([cos, cos], axis=-1)
    sin_h = jnp.concatenate([-sin, sin], axis=-1)
    return jnp.tile(cos_h, (1, N_HEADS)), jnp.tile(sin_h, (1, N_HEADS))


def _ret_tables():
    log_gamma = jnp.log1p(-jnp.exp2(-5.0 - jnp.arange(N_HEADS, dtype=f32)))
    lg = _rep_head(log_gamma)[None, :]
    g = jnp.cumsum(jnp.broadcast_to(lg, (CHUNK, GW)), axis=0)
    g_h = g[:, ::HEAD_DIM]
    diff = g_h[:, None, :] - g_h[None, :, :]
    incl = jnp.tril(jnp.ones((CHUNK, CHUNK), dtype=bool))[:, :, None]
    dec = jnp.exp(jnp.where(incl, diff, -jnp.inf))
    dec = jnp.transpose(dec, (0, 2, 1)).reshape(CHUNK, GW)
    qgs = jnp.exp(g)
    kgs = jnp.exp(g[-1:] - g)
    gl = jnp.exp(g[-1:])
    return jnp.stack([dec, qgs, kgs]), gl


def _layer_params(l, p):
    f = lambda a: a[l]
    w_in = f(p["w_in"])
    c = [0, 768, 1024, 1028, 1032, 2056, 2312, 2568, 3336, 3592]
    a_qkv, a_z, a_b, a_a, b_rw, c_x, c_g, d_qkv, d_g = (w_in[:, c[i]:c[i + 1]] for i in range(9))
    rep = lambda w: jnp.repeat(w, HEAD_DIM, axis=1)
    w_all = jnp.concatenate([a_qkv, a_z, rep(a_b), rep(a_a), b_rw, c_x, c_g, d_qkv, d_g], axis=1).astype(bf16)
    lp = dict(
        norm_mix_w=f(p["norm_mix_w"])[None, :],
        w_all=w_all,
        gdn_cw=f(p["gdn_conv_w"]),
        gdn_vec=_rows8(-jnp.exp(_rep_head(f(p["gdn_a_log"]))), _rep_head(f(p["gdn_dt_bias"])),
                       jnp.tile(f(p["gdn_norm_w"]), N_HEADS)),
        rwkv_vec=_rows8(f(p["rwkv_w0"]), f(p["rwkv_a0"]), f(p["rwkv_k_k"]), f(p["rwkv_k_a"]),
                        f(p["rwkv_r_k"]).reshape(-1), f(p["rwkv_ln_w"]), f(p["rwkv_ln_b"])),
        rwkv_mu=f(p["rwkv_mu"])[None, :],
        rwkv_wup=f(p["rwkv_w_up"]), rwkv_aup=f(p["rwkv_a_up"]), rwkv_gup=f(p["rwkv_g_up"]),
        lru_cw=f(p["lru_conv_w"]),
        lru_vec=_rows8(f(p["lru_conv_b"]), f(p["lru_ba"]), f(p["lru_bx"]),
                       jax.nn.softplus(-f(p["lru_lambda"]))),
        lru_wa=_block_diag4(f(p["lru_wa"])).astype(bf16),
        lru_wx=_block_diag4(f(p["lru_wx"])).astype(bf16),
        ret_norm=(f(p["ret_norm_w"]), f(p["ret_norm_b"])),
        w_out=f(p["w_out"]).astype(bf16),
        norm_ffn_w=f(p["norm_ffn_w"])[None, :],
        w_router=jnp.concatenate([f(p["moe_router_e"]), f(p["moe_router_g"]),
                                  jnp.zeros((D_MODEL, 128 - N_EXPERTS - N_GROUPS), f32)], axis=1),
        b_router=jnp.concatenate([f(p["moe_router_e_b"]), f(p["moe_router_g_b"]),
                                  jnp.zeros((128 - N_EXPERTS - N_GROUPS,), f32)])[None, :],
        moe_wg=f(p["moe_w_gate"]).astype(bf16),
        moe_wu=f(p["moe_w_up"]).astype(bf16),
        moe_wd=f(p["moe_w_down"]).astype(bf16),
    )
    return lp


def _trunk(x, states, pos, layer_ps, norm_final_w):
    bsz, length, _ = x.shape
    t = bsz * length
    tm = min(512, t)
    assert t % tm == 0 and bsz % SEQ_PER_STEP == 0 and length % CHUNK == 0
    cos, sin = _rope_tables(pos)
    ret_tab, ret_gl = _ret_tables()
    new = {k: [] for k in ("gdn", "gdn_conv", "rwkv", "rwkv_shift", "lru", "lru_conv", "ret")}
    x2d = x.reshape(t, D_MODEL)
    for l, lp in enumerate(layer_ps):
        pa, pb, pc, pd = _proj(x2d, t, lp["norm_mix_w"], lp["w_all"], tm)
        pa = pa.reshape(bsz, length, PA_W)
        pb = pb.reshape(bsz, length, PB_W)
        pc = pc.reshape(bsz, length, PC_W)
        pd = pd.reshape(bsz, length, PD_W)
        o_a, s_gdn = _gdn(pa, _pad_hist(states["gdn_conv"][l]), _to_bd(states["gdn"][l]),
                          lp["gdn_cw"], lp["gdn_vec"])
        o_b, s_rwkv = _rwkv(pb, states["rwkv_shift"][l], _to_bd(states["rwkv"][l]), lp["rwkv_vec"],
                            lp["rwkv_mu"], lp["rwkv_wup"], lp["rwkv_aup"], lp["rwkv_gup"])
        o_c, h_lru = _lru(pc, _pad_hist(states["lru_conv"][l]), states["lru"][l][:, None, :],
                          lp["lru_cw"], lp["lru_vec"], lp["lru_wa"], lp["lru_wx"])
        o_d, s_ret = _ret(pd, _to_bd(states["ret"][l]), cos, sin, ret_tab,
                          _rows8(ret_gl, lp["ret_norm"][0], lp["ret_norm"][1]))
        outs = [o.reshape(t, GW) for o in (o_a, o_b, o_c, o_d)]
        x1g = _post(x2d, outs, lp["w_out"], lp["norm_ffn_w"], lp["w_router"], lp["b_router"], tm)
        x2d = _moe(x1g, *_route(x1g, tm), lp["moe_wg"], lp["moe_wu"], lp["moe_wd"], lp["norm_ffn_w"],
                   norm_final_w[None, :], tm, final_norm=(l == len(layer_ps) - 1))
        new["gdn"].append(_from_bd(s_gdn))
        new["gdn_conv"].append(pa[:, length - 3:, 0:3 * GW])
        new["rwkv"].append(_from_bd(s_rwkv))
        new["rwkv_shift"].append(pb[:, length - 1:, :])
        new["lru"].append(h_lru[:, 0, :])
        new["lru_conv"].append(pc[:, length - 3:, 0:GW])
        new["ret"].append(_from_bd(s_ret))
    return x2d[:t].reshape(bsz, length, D_MODEL), {k: jnp.stack(v) for k, v in new.items()}


def _zero_states(bsz, dtype):
    return {"gdn": jnp.zeros((DEPTH, bsz, N_HEADS, HEAD_DIM, HEAD_DIM), dtype),
            "gdn_conv": jnp.zeros((DEPTH, bsz, 3, 3 * GW), dtype),
            "rwkv": jnp.zeros((DEPTH, bsz, N_HEADS, HEAD_DIM, HEAD_DIM), dtype),
            "rwkv_shift": jnp.zeros((DEPTH, bsz, 1, PB_W), dtype),
            "lru": jnp.zeros((DEPTH, bsz, GW), dtype),
            "lru_conv": jnp.zeros((DEPTH, bsz, 3, GW), dtype),
            "ret": jnp.zeros((DEPTH, bsz, N_HEADS, HEAD_DIM, HEAD_DIM), dtype)}


def kernel(x_prompt, x_sample, state_gdn, state_gdn_conv, state_rwkv, state_rwkv_shift, state_lru, state_lru_conv, state_ret, norm_mix_w, w_in, gdn_conv_w, gdn_a_log, gdn_dt_bias, gdn_norm_w, rwkv_mu, rwkv_w0, rwkv_w_up, rwkv_a0, rwkv_a_up, rwkv_g_up, rwkv_k_k, rwkv_k_a, rwkv_r_k, rwkv_ln_w, rwkv_ln_b, lru_conv_w, lru_conv_b, lru_wa, lru_ba, lru_wx, lru_bx, lru_lambda, ret_norm_w, ret_norm_b, w_out, norm_ffn_w, moe_router_g, moe_router_g_b, moe_router_e, moe_router_e_b, moe_w_gate, moe_w_up, moe_w_down, norm_final_w):
    p = dict(norm_mix_w=norm_mix_w, w_in=w_in, gdn_conv_w=gdn_conv_w, gdn_a_log=gdn_a_log,
             gdn_dt_bias=gdn_dt_bias, gdn_norm_w=gdn_norm_w, rwkv_mu=rwkv_mu, rwkv_w0=rwkv_w0,
             rwkv_w_up=rwkv_w_up, rwkv_a0=rwkv_a0, rwkv_a_up=rwkv_a_up, rwkv_g_up=rwkv_g_up,
             rwkv_k_k=rwkv_k_k, rwkv_k_a=rwkv_k_a, rwkv_r_k=rwkv_r_k, rwkv_ln_w=rwkv_ln_w,
             rwkv_ln_b=rwkv_ln_b, lru_conv_w=lru_conv_w, lru_conv_b=lru_conv_b, lru_wa=lru_wa,
             lru_ba=lru_ba, lru_wx=lru_wx, lru_bx=lru_bx, lru_lambda=lru_lambda,
             ret_norm_w=ret_norm_w, ret_norm_b=ret_norm_b, w_out=w_out, norm_ffn_w=norm_ffn_w,
             moe_router_g=moe_router_g, moe_router_g_b=moe_router_g_b, moe_router_e=moe_router_e,
             moe_router_e_b=moe_router_e_b, moe_w_gate=moe_w_gate, moe_w_up=moe_w_up,
             moe_w_down=moe_w_down)
    depth = w_in.shape[0]
    layer_ps = [_layer_params(l, p) for l in range(depth)]
    pos_p = jnp.arange(x_prompt.shape[1], dtype=jnp.int32)
    y_p, new_p = _trunk(x_prompt, _zero_states(x_prompt.shape[0], x_prompt.dtype), pos_p, layer_ps, norm_final_w)
    states_s = {"gdn": state_gdn, "gdn_conv": state_gdn_conv, "rwkv": state_rwkv,
                "rwkv_shift": state_rwkv_shift, "lru": state_lru, "lru_conv": state_lru_conv,
                "ret": state_ret}
    pos_s = PAST_LEN + jnp.arange(x_sample.shape[1], dtype=jnp.int32)
    y_s, new_s = _trunk(x_sample, states_s, pos_s, layer_ps, norm_final_w)
    return (y_p, y_s,
            new_p["gdn"], new_s["gdn"], new_p["gdn_conv"], new_s["gdn_conv"],
            new_p["rwkv"], new_s["rwkv"], new_p["rwkv_shift"], new_s["rwkv_shift"],
            new_p["lru"], new_s["lru"], new_p["lru_conv"], new_s["lru_conv"],
            new_p["ret"], new_s["ret"])
```

```python
import functools

import jax
import jax.numpy as jnp
from jax import lax
from jax.experimental import pallas as pl
from jax.experimental.pallas import tpu as pltpu

f32 = jnp.float32
bf16 = jnp.bfloat16
HIGHEST = lax.Precision.HIGHEST

D_MODEL = 1024
N_HEADS = 4
HEAD_DIM = 64
GW = N_HEADS * HEAD_DIM
CHUNK = 64
DEPTH = 2
PAST_LEN = 4096
RWKV_W_SCALE = 0.606531
RWKV_LN_EPS = 64e-5
LRU_C = 8.0
ROPE_BASE = 10000.0
N_GROUPS = 4
EXPERTS_PER_GROUP = 4
N_EXPERTS = 16
D_EXPERT = 512

PA_W = 3 * GW + GW + 2 * GW
PB_W = 1024
PC_W = 2 * GW
PD_W = 3 * GW + GW
P_ALL = PA_W + PB_W + PC_W + PD_W

XG_W = D_MODEL + 128
GRP_LANE = 16

VMEM_LIMIT_BYTES = 56 * 1024 * 1024
SEQ_PER_STEP = 4

NN = (((1,), (0,)), ((), ()))
NT = (((1,), (1,)), ((), ()))
TN = (((0,), (0,)), ((), ()))


def _dot(a, b, dims=NN, precision=None):
    return lax.dot_general(a, b, dims, precision=precision, preferred_element_type=f32)


def _p1(x):
    return (x.astype(bf16),)


def _p3(x):
    hi = x.astype(bf16)
    return (hi, (x - hi.astype(f32)).astype(bf16))


def _mm(ap, bp, dims=NN):
    out = _dot(ap[0], bp[0], dims)
    if len(bp) > 1:
        out = out + _dot(ap[0], bp[1], dims)
    if len(ap) > 1:
        out = out + _dot(ap[1], bp[0], dims)
    return out


def _iota(shape, dim):
    return lax.broadcasted_iota(jnp.int32, shape, dim)


def _masks(nb):
    rows = nb * CHUNK
    t = _iota((rows, GW), 0) & (CHUNK - 1)
    lane = _iota((rows, GW), 1)
    s = lane & (HEAD_DIM - 1)
    r2 = _iota((GW, GW), 0) >> 6
    c2 = _iota((GW, GW), 1) >> 6
    bd = r2 == c2
    ts = _iota((CHUNK, CHUNK), 0)
    ss = _iota((CHUNK, CHUNK), 1)
    return dict(
        t=t, lane=lane, s=s, bd=bd,
        incl=t >= s, strict=t > s,
        blk16=(t >> 4) == (s >> 4), blk32=(t >> 5) == (s >> 5),
        bones=jnp.where(bd, 1.0, 0.0).astype(bf16),
        lt=jnp.where(ts >= ss, 1.0, 0.0).astype(bf16),
    )


def _rs(x, b):
    return x[b * CHUNK:(b + 1) * CHUNK]


def _rsp(parts, b):
    return tuple(_rs(p, b) for p in parts)


def _each(nb, fn):
    return jnp.concatenate([fn(b) for b in range(nb)], axis=0)


def _bd(parts, m):
    return tuple(
        jnp.where(m["bd"], jnp.concatenate([p] * N_HEADS, axis=0), jnp.zeros((), p.dtype))
        for p in parts)


def _mm_bd(lp, rp, m, nb, dims=NN):
    return _each(nb, lambda b: _mm(_rsp(lp, b), _bd(_rsp(rp, b), m), dims))


def _segsum(x, m):
    return _mm(_p3(x), (m["bones"],))


def _p_exact(x):
    p1 = x.astype(bf16)
    r1 = x - p1.astype(f32)
    p2 = r1.astype(bf16)
    return (p1, p2, (r1 - p2.astype(f32)).astype(bf16))


def _cumsum_t(x, m, nb):
    parts = _p_exact(x)

    def one(b):
        acc = _dot(m["lt"], _rs(parts[0], b))
        for p in parts[1:]:
            acc = acc + _dot(m["lt"], _rs(p, b))
        return acc

    return _each(nb, one)


def _last_row(x, nb):
    return _each(nb, lambda b: jnp.broadcast_to(_rs(x, b)[CHUNK - 1:CHUNK], (CHUNK, x.shape[1])))


def _softplus(x):
    return jnp.maximum(x, 0.0) + jnp.log1p(jnp.exp(-jnp.abs(x)))


def _tri_inv_q(a, m, nb, prep):
    mm = lambda x, y: _mm_bd(prep(x), prep(y), m, nb)
    m1 = -jnp.where(m["blk16"], a, 0.0)
    m2 = mm(m1, m1)
    m4 = mm(m2, m2)
    m8 = mm(m4, m4)
    q = m1
    q = q + m2 + mm(q, m2)
    q = q + m4 + mm(q, m4)
    q = q + m8 + mm(q, m8)
    for e in (jnp.where(m["blk32"] & jnp.logical_not(m["blk16"]), a, 0.0),
              jnp.where(m["blk32"], 0.0, a)):
        x = e + mm(q, e)
        y = x + mm(x, q)
        q = q - y
    return q


PREP_INV = _p1
PREP = _p1


def _cparams(sem):
    return pltpu.CompilerParams(dimension_semantics=sem, vmem_limit_bytes=VMEM_LIMIT_BYTES)


def _proj_kernel(x_ref, nw_ref, w_ref, pa_ref, pb_ref, pc_ref, pd_ref):
    x = x_ref[...]
    h = (x * lax.rsqrt(jnp.mean(x * x, axis=-1, keepdims=True) + 1e-6) * nw_ref[...]).astype(bf16)
    off = 0
    for ref, width in ((pa_ref, PA_W), (pb_ref, PB_W), (pc_ref, PC_W), (pd_ref, PD_W)):
        ref[...] = _dot(h, w_ref[:, off:off + width])
        off += width


def _proj(x2d, t, nw, w_all, tm):
    widths = (PA_W, PB_W, PC_W, PD_W)
    return pl.pallas_call(
        _proj_kernel,
        grid=(t // tm,),
        in_specs=[pl.BlockSpec((tm, D_MODEL), lambda i: (i, 0)),
                  pl.BlockSpec((1, D_MODEL), lambda i: (0, 0)),
                  pl.BlockSpec((D_MODEL, P_ALL), lambda i: (0, 0))],
        out_specs=[pl.BlockSpec((tm, w), lambda i: (i, 0)) for w in widths],
        out_shape=[jax.ShapeDtypeStruct((t, w), f32) for w in widths],
        compiler_params=_cparams(("parallel",)),
        name="proj",
    )(x2d, nw, w_all)


def _conv4(xb_ref, rows, cw):
    y = xb_ref[:, 5:5 + rows, :] * cw[0:1, :]
    for j in range(1, 4):
        y = y + xb_ref[:, 5 + j:5 + j + rows, :] * cw[j:j + 1, :]
    return y


def _gdn_kernel(pa_ref, cb_ref, s0_ref, cw_ref, vec_ref, o_ref, sout_ref, xb_ref, s_ref, *, nb):
    c = pl.program_id(1)
    rows = nb * CHUNK
    m = _masks(nb)

    @pl.when(c == 0)
    def _():
        s_ref[...] = s0_ref[...]
        xb_ref[:, 0:8, :] = cb_ref[...]

    vec = vec_ref[...]
    neg_exp_alog, dt_bias, norm_w = vec[0:1, :], vec[1:2, :], vec[2:3, :]

    xb_ref[:, 8:8 + CHUNK, :] = pa_ref[:, :, 0:3 * GW]
    y = _conv4(xb_ref, CHUNK, cw_ref[...])
    xb_ref[:, 0:8, :] = xb_ref[:, CHUNK:CHUNK + 8, :]
    y = y.reshape(rows, 3 * GW)
    cq = y * jax.nn.sigmoid(y)
    q, k, v = cq[:, 0:GW], cq[:, GW:2 * GW], cq[:, 2 * GW:3 * GW]
    z = pa_ref[:, :, 3 * GW:4 * GW].reshape(rows, GW)
    b_raw = pa_ref[:, :, 4 * GW:5 * GW].reshape(rows, GW)
    a_raw = pa_ref[:, :, 5 * GW:6 * GW].reshape(rows, GW)
    q = q * lax.rsqrt(_segsum(q * q, m) + 1e-6) * (HEAD_DIM ** -0.5)
    k = k * lax.rsqrt(_segsum(k * k, m) + 1e-6)
    beta = jax.nn.sigmoid(b_raw)
    g = neg_exp_alog * _softplus(a_raw + dt_bias)
    cum = _cumsum_t(jnp.concatenate([g, jnp.where(m["strict"], g, 0.0)], axis=1), m, nb)
    gc = cum[:, 0:GW]
    decay = jnp.exp(jnp.where(m["incl"], cum[:, GW:2 * GW], -jnp.inf))
    eg = jnp.exp(gc)
    glast = _last_row(gc, nb)
    kb = k * beta
    k_p = PREP(k)
    a = jnp.where(m["strict"], _mm_bd(PREP(kb), k_p, m, nb, NT) * decay, 0.0)
    qt_p = PREP_INV(_tri_inv_q(a, m, nb, PREP_INV))
    rhs_w = kb * eg
    rhs_u = v * beta
    w = rhs_w + _mm_bd(qt_p, PREP_INV(rhs_w), m, nb)
    u = rhs_u + _mm_bd(qt_p, PREP_INV(rhs_u), m, nb)
    qk = _mm_bd(PREP(q), k_p, m, nb, NT) * decay
    qg_p = PREP(q * eg)
    kg_p = PREP(k * jnp.exp(glast - gc))
    w_p = PREP(w)
    s_old = [s_ref[b] for b in range(nb)]
    s_p = [PREP(s) for s in s_old]
    vn = u - _each(nb, lambda b: _mm(_rsp(w_p, b), s_p[b]))
    vn_p = PREP(vn)
    o = _each(nb, lambda b: _mm(_rsp(qg_p, b), s_p[b])) + _mm_bd(PREP(qk), vn_p, m, nb)
    for b in range(nb):
        upd = _mm(_rsp(kg_p, b), _rsp(vn_p, b), TN)
        s_ref[b] = s_old[b] * jnp.exp(_rs(gc, b)[CHUNK - 1:CHUNK]) + jnp.where(m["bd"], upd, 0.0)
    o = o * lax.rsqrt(_segsum(o * o, m) * (1.0 / HEAD_DIM) + 1e-6) * norm_w * (z * jax.nn.sigmoid(z))
    o_ref[...] = o.reshape(nb, CHUNK, GW)

    @pl.when(c == pl.num_programs(1) - 1)
    def _():
        sout_ref[...] = s_ref[...]


def _gdn(pa, cb8, s0_bd, cw, vec):
    bsz, length, _ = pa.shape
    nb = SEQ_PER_STEP
    grid = (bsz // nb, length // CHUNK)
    return pl.pallas_call(
        functools.partial(_gdn_kernel, nb=nb),
        grid=grid,
        in_specs=[pl.BlockSpec((nb, CHUNK, PA_W), lambda i, c: (i, c, 0)),
                  pl.BlockSpec((nb, 8, 3 * GW), lambda i, c: (i, 0, 0)),
                  pl.BlockSpec((nb, GW, GW), lambda i, c: (i, 0, 0)),
                  pl.BlockSpec((4, 3 * GW), lambda i, c: (0, 0)),
                  pl.BlockSpec((8, GW), lambda i, c: (0, 0))],
        out_specs=[pl.BlockSpec((nb, CHUNK, GW), lambda i, c: (i, c, 0)),
                   pl.BlockSpec((nb, GW, GW), lambda i, c: (i, 0, 0))],
        out_shape=[jax.ShapeDtypeStruct((bsz, length, GW), f32),
                   jax.ShapeDtypeStruct((bsz, GW, GW), f32)],
        scratch_shapes=[pltpu.VMEM((nb, CHUNK + 8, 3 * GW), f32),
                        pltpu.VMEM((nb, GW, GW), f32)],
        compiler_params=_cparams(("arbitrary", "arbitrary")),
        name="gdn",
    )(pa, cb8, s0_bd, cw, vec)


def _rwkv_kernel(pb_ref, sh_ref, s0_ref, vec_ref, mu_ref, wup_ref, aup_ref, gup_ref,
                 o_ref, sout_ref, prev_ref, s_ref, *, nb):
    c = pl.program_id(1)
    rows = nb * CHUNK
    m = _masks(nb)

    @pl.when(c == 0)
    def _():
        s_ref[...] = s0_ref[...]
        prev_ref[...] = sh_ref[...]

    vec = vec_ref[...]
    w0, a0, k_k, k_a, r_k, ln_w, ln_b = (vec[i:i + 1, :] for i in range(7))

    p = pb_ref[...].reshape(rows, PB_W)
    first = (_iota((rows, PB_W), 0) & (CHUNK - 1)) == 0
    carried = _each(nb, lambda b: jnp.broadcast_to(prev_ref[b], (CHUNK, PB_W)))
    prev = jnp.where(first, carried, pltpu.roll(p, 1, 0))
    for b in range(nb):
        prev_ref[b] = _rs(p, b)[CHUNK - 1:CHUNK, :]
    xs = p + (prev - p) * mu_ref[...]
    r, k, v = xs[:, 0:GW], xs[:, GW:2 * GW], xs[:, 2 * GW:3 * GW]
    wd, ad, gd = xs[:, 768:832], xs[:, 832:896], xs[:, 896:1024]
    logw = -RWKV_W_SCALE * jax.nn.sigmoid(w0 + _mm(_p3(jnp.tanh(wd)), _p3(wup_ref[...])))
    a = jax.nn.sigmoid(a0 + _mm(_p3(ad), _p3(aup_ref[...])))
    gate = _mm(_p3(jax.nn.sigmoid(gd)), _p3(gup_ref[...]))
    kk = k * k_k
    kk = kk * lax.rsqrt(_segsum(kk * kk, m) + 1e-6)
    k = k * (1.0 + (a - 1.0) * k_a)

    gc = _cumsum_t(logw, m, nb)
    glast = _last_row(gc, nb)
    eng = jnp.exp(-gc)
    kkd = kk * jnp.exp(gc - logw)
    bvec = a * kk
    rd = r * jnp.exp(gc)
    kkd_p, rd_p = PREP(kkd), PREP(rd)
    binv_p, kinv_p, v_p = PREP(bvec * eng), PREP(k * eng), PREP(v)

    def intra(b):
        lhs = tuple(jnp.concatenate([_rs(x, b), _rs(y, b)], axis=0) for x, y in zip(kkd_p, rd_p))
        return jnp.concatenate([_mm(lhs, _bd(_rsp(binv_p, b), m), NT),
                                _mm(lhs, _bd(_rsp(kinv_p, b), m), NT)], axis=1)

    prods = [intra(b) for b in range(nb)]
    top = jnp.concatenate([x[0:CHUNK] for x in prods], axis=0)
    bot = jnp.concatenate([x[CHUNK:] for x in prods], axis=0)
    amat = jnp.where(m["strict"], top[:, 0:GW], 0.0)
    bmat = jnp.where(m["strict"], top[:, GW:], 0.0)
    rb = jnp.where(m["incl"], bot[:, 0:GW], 0.0)
    rk = jnp.where(m["incl"], bot[:, GW:], 0.0)
    qt_p = PREP_INV(_tri_inv_q(amat, m, nb, PREP_INV))

    s_old = [s_ref[b] for b in range(nb)]
    s_p = [PREP(s) for s in s_old]
    rhs = _each(nb, lambda b: _mm(_rsp(kkd_p, b), s_p[b], NT)) + _mm_bd(PREP(bmat), v_p, m, nb)
    u = -(rhs + _mm_bd(qt_p, PREP_INV(rhs), m, nb))
    u_p = PREP(u)
    o = (_each(nb, lambda b: _mm(_rsp(rd_p, b), s_p[b], NT))
         + _mm_bd(PREP(rb), u_p, m, nb) + _mm_bd(PREP(rk), v_p, m, nb))
    dec_end = jnp.exp(glast - gc)
    bend_p, kend_p = PREP(bvec * dec_end), PREP(k * dec_end)
    for b in range(nb):
        lhs = tuple(jnp.concatenate([_rs(x, b), _rs(y, b)], axis=0) for x, y in zip(u_p, v_p))
        rhs2 = tuple(jnp.concatenate([_rs(x, b), _rs(y, b)], axis=0) for x, y in zip(bend_p, kend_p))
        upd = _mm(lhs, rhs2, TN)
        s_ref[b] = s_old[b] * jnp.exp(_rs(gc, b)[CHUNK - 1:CHUNK]) + jnp.where(m["bd"], upd, 0.0)

    mean = _segsum(o, m) * (1.0 / HEAD_DIM)
    oc = o - mean
    var = _segsum(oc * oc, m) * (1.0 / HEAD_DIM)
    on = oc * lax.rsqrt(var + RWKV_LN_EPS) * ln_w + ln_b
    bonus = _segsum(r * k * r_k, m) * v
    o_ref[...] = ((on + bonus) * gate).reshape(nb, CHUNK, GW)

    @pl.when(c == pl.num_programs(1) - 1)
    def _():
        sout_ref[...] = s_ref[...]


def _rwkv(pb, shift, s0_bd, vec, mu, wup, aup, gup):
    bsz, length, _ = pb.shape
    nb = SEQ_PER_STEP
    grid = (bsz // nb, length // CHUNK)
    full = lambda shape: pl.BlockSpec(shape, lambda i, c: tuple(0 for _ in shape))
    return pl.pallas_call(
        functools.partial(_rwkv_kernel, nb=nb),
        grid=grid,
        in_specs=[pl.BlockSpec((nb, CHUNK, PB_W), lambda i, c: (i, c, 0)),
                  pl.BlockSpec((nb, 1, PB_W), lambda i, c: (i, 0, 0)),
                  pl.BlockSpec((nb, GW, GW), lambda i, c: (i, 0, 0)),
                  full((8, GW)), full((1, PB_W)), full((64, GW)), full((64, GW)), full((128, GW))],
        out_specs=[pl.BlockSpec((nb, CHUNK, GW), lambda i, c: (i, c, 0)),
                   pl.BlockSpec((nb, GW, GW), lambda i, c: (i, 0, 0))],
        out_shape=[jax.ShapeDtypeStruct((bsz, length, GW), f32),
                   jax.ShapeDtypeStruct((bsz, GW, GW), f32)],
        scratch_shapes=[pltpu.VMEM((nb, 1, PB_W), f32),
                        pltpu.VMEM((nb, GW, GW), f32)],
        compiler_params=_cparams(("arbitrary", "arbitrary")),
        name="rwkv",
    )(pb, shift, s0_bd, vec, mu, wup, aup, gup)


def _neg_expm1(x):
    u = jnp.exp(x)
    um1 = u - 1.0
    lu = jnp.log(u)
    safe = jnp.where(um1 == 0.0, x, um1 * x / jnp.where(lu == 0.0, 1.0, lu))
    return -jnp.where(x < -0.5, um1, safe)


def _lru_kernel(pc_ref, cb_ref, h0_ref, cw_ref, vec_ref, wa_ref, wx_ref,
                o_ref, hout_ref, xb_ref, h_ref, *, nb, rows):
    c = pl.program_id(1)

    @pl.when(c == 0)
    def _():
        h_ref[...] = h0_ref[...]
        xb_ref[:, 0:8, :] = cb_ref[...]

    cw = cw_ref[...]
    vec = vec_ref[...]
    conv_b, ba, bx, sp_neg_lam = (vec[i:i + 1, :] for i in range(4))
    t = _iota((rows, GW), 0)

    xb_ref[:, 8:8 + rows, :] = pc_ref[:, :, 0:GW]
    xc_all = _conv4(xb_ref, rows, cw) + conv_b
    xb_ref[:, 0:8, :] = xb_ref[:, rows:rows + 8, :]
    for b in range(nb):
        xc = xc_all[b]
        gb = pc_ref[b, :, GW:2 * GW]
        xcp = PREP(xc)
        r = jax.nn.sigmoid(_mm(xcp, (wa_ref[...],)) + ba)
        i = jax.nn.sigmoid(_mm(xcp, (wx_ref[...],)) + bx)
        log_a = -LRU_C * r * sp_neg_lam
        a = jnp.exp(log_a)
        bt = jnp.sqrt(_neg_expm1(2.0 * log_a)) * (i * xc)
        d = 1
        while d < rows:
            keep = t >= d
            a_sh = jnp.where(keep, pltpu.roll(a, d, 0), 1.0)
            b_sh = jnp.where(keep, pltpu.roll(bt, d, 0), 0.0)
            bt = a * b_sh + bt
            a = a * a_sh
            d *= 2
        h = bt + a * h_ref[b]
        h_ref[b] = h[rows - 1:rows, :]
        o_ref[b] = h * jax.nn.gelu(gb)

    @pl.when(c == pl.num_programs(1) - 1)
    def _():
        hout_ref[...] = h_ref[...]


def _lru(pc, cb8, h0, cw, vec, wa_bd, wx_bd):
    bsz, length, _ = pc.shape
    nb = SEQ_PER_STEP
    rows = min(length, 256)
    grid = (bsz // nb, length // rows)
    full = lambda shape: pl.BlockSpec(shape, lambda i, c: tuple(0 for _ in shape))
    return pl.pallas_call(
        functools.partial(_lru_kernel, nb=nb, rows=rows),
        grid=grid,
        in_specs=[pl.BlockSpec((nb, rows, PC_W), lambda i, c: (i, c, 0)),
                  pl.BlockSpec((nb, 8, GW), lambda i, c: (i, 0, 0)),
                  pl.BlockSpec((nb, 1, GW), lambda i, c: (i, 0, 0)),
                  full((4, GW)), full((8, GW)), full((GW, GW)), full((GW, GW))],
        out_specs=[pl.BlockSpec((nb, rows, GW), lambda i, c: (i, c, 0)),
                   pl.BlockSpec((nb, 1, GW), lambda i, c: (i, 0, 0))],
        out_shape=[jax.ShapeDtypeStruct((bsz, length, GW), f32),
                   jax.ShapeDtypeStruct((bsz, 1, GW), f32)],
        scratch_shapes=[pltpu.VMEM((nb, rows + 8, GW), f32),
                        pltpu.VMEM((nb, 1, GW), f32)],
        compiler_params=_cparams(("arbitrary", "arbitrary")),
        name="lru",
    )(pc, cb8, h0, cw, vec, wa_bd, wx_bd)


def _ret_kernel(pd_ref, r0_ref, cos_ref, sin_ref, tab_ref, vec_ref, o_ref, rout_ref, s_ref, *, nb):
    c = pl.program_id(1)
    rows = nb * CHUNK
    m = _masks(nb)

    @pl.when(c == 0)
    def _():
        s_ref[...] = r0_ref[...]

    tile = lambda x: jnp.concatenate([x] * nb, axis=0)
    cos, sin = tile(cos_ref[...]), tile(sin_ref[...])
    dec, qgs, kgs = tile(tab_ref[0]), tile(tab_ref[1]), tile(tab_ref[2])
    vec = vec_ref[...]
    gl, norm_w, norm_b = vec[0:1, :], vec[1:2, :], vec[2:3, :]
    low_half = m["s"] < (HEAD_DIM // 2)

    def rot(x):
        swapped = jnp.where(low_half, pltpu.roll(x, GW - HEAD_DIM // 2, 1), pltpu.roll(x, HEAD_DIM // 2, 1))
        return x * cos + swapped * sin

    q = rot(pd_ref[:, :, 0:GW].reshape(rows, GW))
    k = rot(pd_ref[:, :, GW:2 * GW].reshape(rows, GW)) * (HEAD_DIM ** -0.5)
    v = pd_ref[:, :, 2 * GW:3 * GW].reshape(rows, GW)
    gate = pd_ref[:, :, 3 * GW:4 * GW].reshape(rows, GW)
    v_p = PREP(v)
    qk = _mm_bd(PREP(q), PREP(k), m, nb, NT) * dec
    qg_p, kg_p = PREP(q * qgs), PREP(k * kgs)
    s_old = [s_ref[b] for b in range(nb)]
    o = _each(nb, lambda b: _mm(_rsp(qg_p, b), PREP(s_old[b]))) + _mm_bd(PREP(qk), v_p, m, nb)
    for b in range(nb):
        s_ref[b] = s_old[b] * gl + jnp.where(m["bd"], _mm(_rsp(kg_p, b), _rsp(v_p, b), TN), 0.0)
    mean = _segsum(o, m) * (1.0 / HEAD_DIM)
    oc = o - mean
    var = _segsum(oc * oc, m) * (1.0 / HEAD_DIM)
    on = oc * lax.rsqrt(var + 1e-5) * norm_w + norm_b
    o_ref[...] = (gate * jax.nn.sigmoid(gate) * on).reshape(nb, CHUNK, GW)

    @pl.when(c == pl.num_programs(1) - 1)
    def _():
        rout_ref[...] = s_ref[...]


def _ret(pd, r0_bd, cos, sin, tab, vec):
    bsz, length, _ = pd.shape
    nb = SEQ_PER_STEP
    grid = (bsz // nb, length // CHUNK)
    full = lambda shape: pl.BlockSpec(shape, lambda i, c: tuple(0 for _ in shape))
    return pl.pallas_call(
        functools.partial(_ret_kernel, nb=nb),
        grid=grid,
        in_specs=[pl.BlockSpec((nb, CHUNK, PD_W), lambda i, c: (i, c, 0)),
                  pl.BlockSpec((nb, GW, GW), lambda i, c: (i, 0, 0)),
                  pl.BlockSpec((CHUNK, GW), lambda i, c: (c, 0)),
                  pl.BlockSpec((CHUNK, GW), lambda i, c: (c, 0)),
                  full((3, CHUNK, GW)), full((8, GW))],
        out_specs=[pl.BlockSpec((nb, CHUNK, GW), lambda i, c: (i, c, 0)),
                   pl.BlockSpec((nb, GW, GW), lambda i, c: (i, 0, 0))],
        out_shape=[jax.ShapeDtypeStruct((bsz, length, GW), f32),
                   jax.ShapeDtypeStruct((bsz, GW, GW), f32)],
        scratch_shapes=[pltpu.VMEM((nb, GW, GW), f32)],
        compiler_params=_cparams(("arbitrary", "arbitrary")),
        name="ret",
    )(pd, r0_bd, cos, sin, tab, vec)


def _post_kernel(x_ref, oa_ref, ob_ref, oc_ref, od_ref, wout_ref, nw_ref, wr_ref, br_ref, x1g_ref):
    acc = x_ref[...]
    for j, ref in enumerate((oa_ref, ob_ref, oc_ref, od_ref)):
        acc = acc + _dot(ref[...].astype(bf16), wout_ref[j * GW:(j + 1) * GW, :])
    x1g_ref[:, 0:D_MODEL] = acc
    tn = acc * lax.rsqrt(jnp.mean(acc * acc, axis=-1, keepdims=True) + 1e-6) * nw_ref[...]
    logits = _mm(_p3(tn), _p3(wr_ref[...])) + br_ref[...]
    lane = _iota(logits.shape, 1)
    lane_f = lane.astype(f32)
    ninf = -jnp.inf
    is_g = (lane >= N_EXPERTS) & (lane < N_EXPERTS + N_GROUPS)
    lg = jnp.where(is_g, logits, ninf)
    gmax = jnp.max(lg, axis=-1, keepdims=True)
    grp_f = jnp.min(jnp.where(lg == gmax, lane_f, 1e9), axis=-1, keepdims=True) - N_EXPERTS
    p_grp = 1.0 / jnp.sum(jnp.exp(lg - gmax), axis=-1, keepdims=True)
    in_grp = (lane >> 2) == grp_f.astype(jnp.int32)
    le = jnp.where(in_grp, logits, ninf)
    v1 = jnp.max(le, axis=-1, keepdims=True)
    i1 = jnp.min(jnp.where(le == v1, lane_f, 1e9), axis=-1, keepdims=True)
    le2 = jnp.where(lane_f == i1, ninf, le)
    v2 = jnp.max(le2, axis=-1, keepdims=True)
    i2 = jnp.min(jnp.where(le2 == v2, lane_f, 1e9), axis=-1, keepdims=True)
    e2 = jnp.exp(v2 - v1)
    wt1 = p_grp / (1.0 + e2)
    wt2 = p_grp * e2 / (1.0 + e2)
    gates = jnp.where(lane_f == i1, wt1, 0.0) + jnp.where(lane_f == i2, wt2, 0.0)
    x1g_ref[:, D_MODEL:] = jnp.where(lane == GRP_LANE, grp_f, gates)


def _post(x2d, outs, wout, nw, wr, br, tm):
    t = outs[0].shape[0]
    row = lambda w: pl.BlockSpec((tm, w), lambda i: (i, 0))
    full = lambda shape: pl.BlockSpec(shape, lambda i: tuple(0 for _ in shape))
    return pl.pallas_call(
        _post_kernel,
        grid=(t // tm,),
        in_specs=[row(D_MODEL), row(GW), row(GW), row(GW), row(GW),
                  full((D_MODEL, D_MODEL)), full((1, D_MODEL)), full((D_MODEL, 128)), full((1, 128))],
        out_specs=row(XG_W),
        out_shape=jax.ShapeDtypeStruct((t, XG_W), f32),
        compiler_params=_cparams(("parallel",)),
        name="post",
    )(x2d, *outs, wout, nw, wr, br)


ROW_DMA_UNROLL = 8


def _route(x1g, tm):
    t = x1g.shape[0]
    n_tiles = t // tm
    n_items = n_tiles + N_GROUPS - 1
    grp = x1g[:, D_MODEL + GRP_LANE].astype(jnp.int32)
    shift = max((t - 1).bit_length(), 1)
    order = jnp.sort((grp << shift) | jnp.arange(t, dtype=jnp.int32)) & ((1 << shift) - 1)
    counts = jnp.sum((grp[:, None] == jnp.arange(N_GROUPS, dtype=jnp.int32)[None, :]).astype(jnp.int32), axis=0)
    inner_ends = jnp.cumsum(counts)[:-1]
    tile_start = jnp.arange(n_tiles, dtype=jnp.int32) * tm
    g_lo = jnp.sum((tile_start[:, None] >= inner_ends[None, :]).astype(jnp.int32), axis=1)
    g_hi = jnp.sum(((tile_start + tm - 1)[:, None] >= inner_ends[None, :]).astype(jnp.int32), axis=1)
    visits = g_hi - g_lo + 1
    item_end = jnp.cumsum(visits)
    w = jnp.arange(n_items, dtype=jnp.int32)
    valid = w < item_end[-1]
    w_tile = jnp.minimum(jnp.sum((w[:, None] >= item_end[None, :]).astype(jnp.int32), axis=1), n_tiles - 1)
    within = w - (item_end - visits)[w_tile]
    w_gid = jnp.where(valid, g_lo[w_tile] + within, g_hi[n_tiles - 1])
    first = valid & (within == 0)
    last = valid & (within == visits[w_tile] - 1)
    i32 = lambda a: a.astype(jnp.int32)
    return order.reshape(n_tiles, 1, tm), i32(w_tile), i32(w_gid), i32(first), i32(last), i32(valid)


def _moe_kernel(tile_ref, gid_ref, first_ref, last_ref, valid_ref, idx_ref, idx_next_ref, x1g_ref,
                wg_ref, wu_ref, wd_ref, nffn_ref, nf_ref, x2_ref, xbuf, acc_ref, obuf, gsem, ssem,
                *, final_norm, tm, n_tiles):
    w = pl.program_id(0)
    tile = tile_ref[w]
    slot = tile % 2
    is_first = first_ref[w] == 1
    is_last = last_ref[w] == 1

    def gather_copy(ref, g, j, s):
        return pltpu.make_async_copy(x1g_ref.at[pl.ds(ref[0, 0, g * 8 + j], 1)],
                                     xbuf.at[s, g, pl.ds(j, 1)], gsem.at[s])

    def scatter_copy(g, j):
        return pltpu.make_async_copy(obuf.at[g, pl.ds(j, 1)],
                                     x2_ref.at[pl.ds(idx_ref[0, 0, g * 8 + j], 1)], ssem)

    def each_row(fn):
        def body(g, carry):
            for j in range(8):
                fn(g, j)
            return carry
        lax.fori_loop(0, tm // 8, body, 0, unroll=ROW_DMA_UNROLL // 8)

    @pl.when(w == 0)
    def _():
        each_row(lambda g, j: gather_copy(idx_ref, g, j, 0).start())

    @pl.when(is_first & (tile + 1 < n_tiles))
    def _():
        each_row(lambda g, j: gather_copy(idx_next_ref, g, j, 1 - slot).start())

    @pl.when(is_first)
    def _():
        each_row(lambda g, j: gather_copy(idx_ref, g, j, slot).wait())
        acc_ref[...] = xbuf[slot].reshape(tm, XG_W)[:, 0:D_MODEL]

    @pl.when(valid_ref[w] == 1)
    def _():
        rows = xbuf[slot].reshape(tm, XG_W)
        x1 = rows[:, 0:D_MODEL]
        gates = rows[:, D_MODEL:]
        tb = (x1 * lax.rsqrt(jnp.mean(x1 * x1, axis=-1, keepdims=True) + 1e-6) * nffn_ref[...]).astype(bf16)
        lane = _iota(gates.shape, 1)
        first_expert = gid_ref[w] * EXPERTS_PER_GROUP
        acc = acc_ref[...]
        for j in range(EXPERTS_PER_GROUP):
            hg = _dot(tb, wg_ref[j])
            hu = _dot(tb, wu_ref[j])
            he = (hg * jax.nn.sigmoid(hg) * hu).astype(bf16)
            ge = jnp.sum(jnp.where(lane == first_expert + j, gates, 0.0), axis=-1, keepdims=True)
            acc = acc + ge * _dot(he, wd_ref[j])
        acc_ref[...] = acc

    @pl.when(is_last)
    def _():
        @pl.when(tile > 0)
        def _():
            each_row(lambda g, j: scatter_copy(g, j).wait())

        out = acc_ref[...]
        if final_norm:
            out = out * lax.rsqrt(jnp.mean(out * out, axis=-1, keepdims=True) + 1e-6) * nf_ref[...]
        obuf[...] = out.reshape(obuf.shape)
        each_row(lambda g, j: scatter_copy(g, j).start())

    @pl.when(w == pl.num_programs(0) - 1)
    def _():
        each_row(lambda g, j: scatter_copy(g, j).wait())


def _moe(x1g, order, w_tile, w_gid, w_first, w_last, w_valid, wg, wu, wd, nffn, nf, tm, final_norm):
    t = x1g.shape[0]
    n_tiles = order.shape[0]
    epg = EXPERTS_PER_GROUP
    smem_tile = lambda fn: pl.BlockSpec((1, 1, tm), fn, memory_space=pltpu.SMEM)
    wspec = lambda shape: pl.BlockSpec(shape, lambda w, tile, gid, *_: (gid[w], 0, 0))
    vec = pl.BlockSpec((1, D_MODEL), lambda w, *_: (0, 0))
    grid_spec = pltpu.PrefetchScalarGridSpec(
        num_scalar_prefetch=5,
        grid=(w_tile.shape[0],),
        in_specs=[smem_tile(lambda w, tile, *_: (tile[w], 0, 0)),
                  smem_tile(lambda w, tile, *_: (jnp.minimum(tile[w] + 1, n_tiles - 1), 0, 0)),
                  pl.BlockSpec(memory_space=pl.ANY),
                  wspec((epg, D_MODEL, D_EXPERT)), wspec((epg, D_MODEL, D_EXPERT)),
                  wspec((epg, D_EXPERT, D_MODEL)), vec, vec],
        out_specs=pl.BlockSpec(memory_space=pl.ANY),
        scratch_shapes=[pltpu.VMEM((2, tm // 8, 8, XG_W), f32), pltpu.VMEM((tm, D_MODEL), f32),
                        pltpu.VMEM((tm // 8, 8, D_MODEL), f32),
                        pltpu.SemaphoreType.DMA((2,)), pltpu.SemaphoreType.DMA],
    )
    return pl.pallas_call(
        functools.partial(_moe_kernel, final_norm=final_norm, tm=tm, n_tiles=n_tiles),
        grid_spec=grid_spec,
        out_shape=jax.ShapeDtypeStruct((t, D_MODEL), f32),
        compiler_params=_cparams(("arbitrary",)),
        name="moe",
    )(w_tile, w_gid, w_first, w_last, w_valid, order, order, x1g, wg, wu, wd, nffn, nf)


def _to_bd(s):
    bsz = s.shape[0]
    eye = jnp.eye(N_HEADS, dtype=s.dtype)
    return jnp.einsum("bhij,hg->bhigj", s, eye).reshape(bsz, GW, GW)


def _from_bd(s):
    bsz = s.shape[0]
    s5 = s.reshape(bsz, N_HEADS, HEAD_DIM, N_HEADS, HEAD_DIM)
    return jnp.stack([s5[:, h, :, h, :] for h in range(N_HEADS)], axis=1)


def _pad_hist(buf):
    return jnp.pad(buf, ((0, 0), (5, 0), (0, 0)))


def _rows8(*rows):
    out = [jnp.reshape(r, (1, -1)).astype(f32) for r in rows]
    width = out[0].shape[1]
    out += [jnp.zeros((1, width), f32)] * (8 - len(out))
    return jnp.concatenate(out, axis=0)


def _rep_head(x):
    return jnp.repeat(x, HEAD_DIM)


def _block_diag4(w):
    eye = jnp.eye(N_HEADS, dtype=w.dtype)
    return jnp.einsum("hij,hg->higj", w, eye).reshape(GW, GW)


def _rope_tables(pos):
    half = HEAD_DIM // 2
    inv = ROPE_BASE ** (-jnp.arange(half, dtype=f32) / half)
    ang = pos.astype(f32)[:, None] * inv
    cos, sin = jnp.cos(ang), jnp.sin(ang)
    cos_h = jnp.concatenate([cos, cos], axis=-1)
    sin_h = jnp.concatenate([-sin, sin], axis=-1)
    return jnp.tile(cos_h, (1, N_HEADS)), jnp.tile(sin_h, (1, N_HEADS))


def _ret_tables():
    log_gamma = jnp.log1p(-jnp.exp2(-5.0 - jnp.arange(N_HEADS, dtype=f32)))
    lg = _rep_head(log_gamma)[None, :]
    g = jnp.cumsum(jnp.broadcast_to(lg, (CHUNK, GW)), axis=0)
    g_h = g[:, ::HEAD_DIM]
    diff = g_h[:, None, :] - g_h[None, :, :]
    incl = jnp.tril(jnp.ones((CHUNK, CHUNK), dtype=bool))[:, :, None]
    dec = jnp.exp(jnp.where(incl, diff, -jnp.inf))
    dec = jnp.transpose(dec, (0, 2, 1)).reshape(CHUNK, GW)
    qgs = jnp.exp(g)
    kgs = jnp.exp(g[-1:] - g)
    gl = jnp.exp(g[-1:])
    return jnp.stack([dec, qgs, kgs]), gl


def _layer_params(l, p):
    f = lambda a: a[l]
    w_in = f(p["w_in"])
    c = [0, 768, 1024, 1028, 1032, 2056, 2312, 2568, 3336, 3592]
    a_qkv, a_z, a_b, a_a, b_rw, c_x, c_g, d_qkv, d_g = (w_in[:, c[i]:c[i + 1]] for i in range(9))
    rep = lambda w: jnp.repeat(w, HEAD_DIM, axis=1)
    w_all = jnp.concatenate([a_qkv, a_z, rep(a_b), rep(a_a), b_rw, c_x, c_g, d_qkv, d_g], axis=1).astype(bf16)
    lp = dict(
        norm_mix_w=f(p["norm_mix_w"])[None, :],
        w_all=w_all,
        gdn_cw=f(p["gdn_conv_w"]),
        gdn_vec=_rows8(-jnp.exp(_rep_head(f(p["gdn_a_log"]))), _rep_head(f(p["gdn_dt_bias"])),
                       jnp.tile(f(p["gdn_norm_w"]), N_HEADS)),
        rwkv_vec=_rows8(f(p["rwkv_w0"]), f(p["rwkv_a0"]), f(p["rwkv_k_k"]), f(p["rwkv_k_a"]),
                        f(p["rwkv_r_k"]).reshape(-1), f(p["rwkv_ln_w"]), f(p["rwkv_ln_b"])),
        rwkv_mu=f(p["rwkv_mu"])[None, :],
        rwkv_wup=f(p["rwkv_w_up"]), rwkv_aup=f(p["rwkv_a_up"]), rwkv_gup=f(p["rwkv_g_up"]),
        lru_cw=f(p["lru_conv_w"]),
        lru_vec=_rows8(f(p["lru_conv_b"]), f(p["lru_ba"]), f(p["lru_bx"]),
                       jax.nn.softplus(-f(p["lru_lambda"]))),
        lru_wa=_block_diag4(f(p["lru_wa"])).astype(bf16),
        lru_wx=_block_diag4(f(p["lru_wx"])).astype(bf16),
        ret_norm=(f(p["ret_norm_w"]), f(p["ret_norm_b"])),
        w_out=f(p["w_out"]).astype(bf16),
        norm_ffn_w=f(p["norm_ffn_w"])[None, :],
        w_router=jnp.concatenate([f(p["moe_router_e"]), f(p["moe_router_g"]),
                                  jnp.zeros((D_MODEL, 128 - N_EXPERTS - N_GROUPS), f32)], axis=1),
        b_router=jnp.concatenate([f(p["moe_router_e_b"]), f(p["moe_router_g_b"]),
                                  jnp.zeros((128 - N_EXPERTS - N_GROUPS,), f32)])[None, :],
        moe_wg=f(p["moe_w_gate"]).astype(bf16),
        moe_wu=f(p["moe_w_up"]).astype(bf16),
        moe_wd=f(p["moe_w_down"]).astype(bf16),
    )
    return lp


def _trunk(x, states, pos, layer_ps, norm_final_w):
    bsz, length, _ = x.shape
    t = bsz * length
    tm = min(512, t)
    assert t % tm == 0 and bsz % SEQ_PER_STEP == 0 and length % CHUNK == 0
    cos, sin = _rope_tables(pos)
    ret_tab, ret_gl = _ret_tables()
    new = {k: [] for k in ("gdn", "gdn_conv", "rwkv", "rwkv_shift", "lru", "lru_conv", "ret")}
    x2d = x.reshape(t, D_MODEL)
    for l, lp in enumerate(layer_ps):
        pa, pb, pc, pd = _proj(x2d, t, lp["norm_mix_w"], lp["w_all"], tm)
        pa = pa.reshape(bsz, length, PA_W)
        pb = pb.reshape(bsz, length, PB_W)
        pc = pc.reshape(bsz, length, PC_W)
        pd = pd.reshape(bsz, length, PD_W)
        o_a, s_gdn = _gdn(pa, _pad_hist(states["gdn_conv"][l]), _to_bd(states["gdn"][l]),
                          lp["gdn_cw"], lp["gdn_vec"])
        o_b, s_rwkv = _rwkv(pb, states["rwkv_shift"][l], _to_bd(states["rwkv"][l]), lp["rwkv_vec"],
                            lp["rwkv_mu"], lp["rwkv_wup"], lp["rwkv_aup"], lp["rwkv_gup"])
        o_c, h_lru = _lru(pc, _pad_hist(states["lru_conv"][l]), states["lru"][l][:, None, :],
                          lp["lru_cw"], lp["lru_vec"], lp["lru_wa"], lp["lru_wx"])
        o_d, s_ret = _ret(pd, _to_bd(states["ret"][l]), cos, sin, ret_tab,
                          _rows8(ret_gl, lp["ret_norm"][0], lp["ret_norm"][1]))
        outs = [o.reshape(t, GW) for o in (o_a, o_b, o_c, o_d)]
        x1g = _post(x2d, outs, lp["w_out"], lp["norm_ffn_w"], lp["w_router"], lp["b_router"], tm)
        x2d = _moe(x1g, *_route(x1g, tm), lp["moe_wg"], lp["moe_wu"], lp["moe_wd"], lp["norm_ffn_w"],
                   norm_final_w[None, :], tm, final_norm=(l == len(layer_ps) - 1))
        new["gdn"].append(_from_bd(s_gdn))
        new["gdn_conv"].append(pa[:, length - 3:, 0:3 * GW])
        new["rwkv"].append(_from_bd(s_rwkv))
        new["rwkv_shift"].append(pb[:, length - 1:, :])
        new["lru"].append(h_lru[:, 0, :])
        new["lru_conv"].append(pc[:, length - 3:, 0:GW])
        new["ret"].append(_from_bd(s_ret))
    return x2d.reshape(bsz, length, D_MODEL), {k: jnp.stack(v) for k, v in new.items()}


def _zero_states(bsz, dtype):
    return {"gdn": jnp.zeros((DEPTH, bsz, N_HEADS, HEAD_DIM, HEAD_DIM), dtype),
            "gdn_conv": jnp.zeros((DEPTH, bsz, 3, 3 * GW), dtype),
            "rwkv": jnp.zeros((DEPTH, bsz, N_HEADS, HEAD_DIM, HEAD_DIM), dtype),
            "rwkv_shift": jnp.zeros((DEPTH, bsz, 1, PB_W), dtype),
            "lru": jnp.zeros((DEPTH, bsz, GW), dtype),
            "lru_conv": jnp.zeros((DEPTH, bsz, 3, GW), dtype),
            "ret": jnp.zeros((DEPTH, bsz, N_HEADS, HEAD_DIM, HEAD_DIM), dtype)}


def kernel(x_prompt, x_sample, state_gdn, state_gdn_conv, state_rwkv, state_rwkv_shift, state_lru, state_lru_conv, state_ret, norm_mix_w, w_in, gdn_conv_w, gdn_a_log, gdn_dt_bias, gdn_norm_w, rwkv_mu, rwkv_w0, rwkv_w_up, rwkv_a0, rwkv_a_up, rwkv_g_up, rwkv_k_k, rwkv_k_a, rwkv_r_k, rwkv_ln_w, rwkv_ln_b, lru_conv_w, lru_conv_b, lru_wa, lru_ba, lru_wx, lru_bx, lru_lambda, ret_norm_w, ret_norm_b, w_out, norm_ffn_w, moe_router_g, moe_router_g_b, moe_router_e, moe_router_e_b, moe_w_gate, moe_w_up, moe_w_down, norm_final_w):
    p = dict(norm_mix_w=norm_mix_w, w_in=w_in, gdn_conv_w=gdn_conv_w, gdn_a_log=gdn_a_log,
             gdn_dt_bias=gdn_dt_bias, gdn_norm_w=gdn_norm_w, rwkv_mu=rwkv_mu, rwkv_w0=rwkv_w0,
             rwkv_w_up=rwkv_w_up, rwkv_a0=rwkv_a0, rwkv_a_up=rwkv_a_up, rwkv_g_up=rwkv_g_up,
             rwkv_k_k=rwkv_k_k, rwkv_k_a=rwkv_k_a, rwkv_r_k=rwkv_r_k, rwkv_ln_w=rwkv_ln_w,
             rwkv_ln_b=rwkv_ln_b, lru_conv_w=lru_conv_w, lru_conv_b=lru_conv_b, lru_wa=lru_wa,
             lru_ba=lru_ba, lru_wx=lru_wx, lru_bx=lru_bx, lru_lambda=lru_lambda,
             ret_norm_w=ret_norm_w, ret_norm_b=ret_norm_b, w_out=w_out, norm_ffn_w=norm_ffn_w,
             moe_router_g=moe_router_g, moe_router_g_b=moe_router_g_b, moe_router_e=moe_router_e,
             moe_router_e_b=moe_router_e_b, moe_w_gate=moe_w_gate, moe_w_up=moe_w_up,
             moe_w_down=moe_w_down)
    depth = w_in.shape[0]
    layer_ps = [_layer_params(l, p) for l in range(depth)]
    pos_p = jnp.arange(x_prompt.shape[1], dtype=jnp.int32)
    y_p, new_p = _trunk(x_prompt, _zero_states(x_prompt.shape[0], x_prompt.dtype), pos_p, layer_ps, norm_final_w)
    states_s = {"gdn": state_gdn, "gdn_conv": state_gdn_conv, "rwkv": state_rwkv,
                "rwkv_shift": state_rwkv_shift, "lru": state_lru, "lru_conv": state_lru_conv,
                "ret": state_ret}
    pos_s = PAST_LEN + jnp.arange(x_sample.shape[1], dtype=jnp.int32)
    y_s, new_s = _trunk(x_sample, states_s, pos_s, layer_ps, norm_final_w)
    return (y_p, y_s,
            new_p["gdn"], new_s["gdn"], new_p["gdn_conv"], new_s["gdn_conv"],
            new_p["rwkv"], new_s["rwkv"], new_p["rwkv_shift"], new_s["rwkv_shift"],
            new_p["lru"], new_s["lru"], new_p["lru_conv"], new_s["lru_conv"],
            new_p["ret"], new_s["ret"])
```

```python
import functools

import jax
import jax.numpy as jnp
from jax import lax
from jax.experimental import pallas as pl
from jax.experimental.pallas import tpu as pltpu

f32 = jnp.float32
bf16 = jnp.bfloat16
HIGHEST = lax.Precision.HIGHEST

D_MODEL = 1024
N_HEADS = 4
HEAD_DIM = 64
GW = N_HEADS * HEAD_DIM
CHUNK = 64
DEPTH = 2
PAST_LEN = 4096
RWKV_W_SCALE = 0.606531
RWKV_LN_EPS = 64e-5
LRU_C = 8.0
ROPE_BASE = 10000.0
N_GROUPS = 4
EXPERTS_PER_GROUP = 4
N_EXPERTS = 16
D_EXPERT = 512

PA_W = 3 * GW + GW + 2 * GW
PB_W = 1024
PC_W = 2 * GW
PD_W = 3 * GW + GW
P_ALL = PA_W + PB_W + PC_W + PD_W

XG_W = D_MODEL + 128
GRP_LANE = 16

VMEM_LIMIT_BYTES = 56 * 1024 * 1024
SEQ_PER_STEP = 4
BLOCKS_PER_STEP = 16

NN = (((1,), (0,)), ((), ()))
NT = (((1,), (1,)), ((), ()))
TN = (((0,), (0,)), ((), ()))


def _dot(a, b, dims=NN, precision=None):
    return lax.dot_general(a, b, dims, precision=precision, preferred_element_type=f32)


def _p1(x):
    return (x.astype(bf16),)


def _p3(x):
    hi = x.astype(bf16)
    return (hi, (x - hi.astype(f32)).astype(bf16))


def _mm(ap, bp, dims=NN):
    out = _dot(ap[0], bp[0], dims)
    if len(bp) > 1:
        out = out + _dot(ap[0], bp[1], dims)
    if len(ap) > 1:
        out = out + _dot(ap[1], bp[0], dims)
    return out


def _iota(shape, dim):
    return lax.broadcasted_iota(jnp.int32, shape, dim)


def _masks(nb):
    rows = nb * CHUNK
    t = _iota((rows, GW), 0) & (CHUNK - 1)
    lane = _iota((rows, GW), 1)
    s = lane & (HEAD_DIM - 1)
    r2 = _iota((GW, GW), 0) >> 6
    c2 = _iota((GW, GW), 1) >> 6
    bd = r2 == c2
    ts = _iota((CHUNK, CHUNK), 0)
    ss = _iota((CHUNK, CHUNK), 1)
    return dict(
        t=t, lane=lane, s=s, bd=bd,
        incl=t >= s, strict=t > s,
        blk16=(t >> 4) == (s >> 4), blk32=(t >> 5) == (s >> 5),
        bones=jnp.where(bd, 1.0, 0.0).astype(bf16),
        lt=jnp.where(ts >= ss, 1.0, 0.0).astype(bf16),
    )


def _rs(x, b):
    return x[b * CHUNK:(b + 1) * CHUNK]


def _rsp(parts, b):
    return tuple(_rs(p, b) for p in parts)


def _each(nb, fn):
    return jnp.concatenate([fn(b) for b in range(nb)], axis=0)


def _bd(parts, m):
    return tuple(
        jnp.where(m["bd"], jnp.concatenate([p] * N_HEADS, axis=0), jnp.zeros((), p.dtype))
        for p in parts)


def _mm_bd(lp, rp, m, nb, dims=NN):
    return _each(nb, lambda b: _mm(_rsp(lp, b), _bd(_rsp(rp, b), m), dims))


def _segsum(x, m):
    return _mm(_p3(x), (m["bones"],))


def _p_exact(x):
    p1 = x.astype(bf16)
    r1 = x - p1.astype(f32)
    p2 = r1.astype(bf16)
    return (p1, p2, (r1 - p2.astype(f32)).astype(bf16))


def _cumsum_t(x, m, nb):
    parts = _p_exact(x)

    def one(b):
        acc = _dot(m["lt"], _rs(parts[0], b))
        for p in parts[1:]:
            acc = acc + _dot(m["lt"], _rs(p, b))
        return acc

    return _each(nb, one)


def _last_row(x, nb):
    return _each(nb, lambda b: jnp.broadcast_to(_rs(x, b)[CHUNK - 1:CHUNK], (CHUNK, x.shape[1])))


def _softplus(x):
    return jnp.maximum(x, 0.0) + jnp.log1p(jnp.exp(-jnp.abs(x)))


def _tri_inv_q(a, m, nb, prep):
    mm = lambda x, y: _mm_bd(prep(x), prep(y), m, nb)
    m1 = -jnp.where(m["blk16"], a, 0.0)
    m2 = mm(m1, m1)
    m4 = mm(m2, m2)
    m8 = mm(m4, m4)
    q = m1
    q = q + m2 + mm(q, m2)
    q = q + m4 + mm(q, m4)
    q = q + m8 + mm(q, m8)
    for e in (jnp.where(m["blk32"] & jnp.logical_not(m["blk16"]), a, 0.0),
              jnp.where(m["blk32"], 0.0, a)):
        x = e + mm(q, e)
        y = x + mm(x, q)
        q = q - y
    return q


PREP_INV = _p1
PREP = _p1


def _cparams(sem):
    return pltpu.CompilerParams(dimension_semantics=sem, vmem_limit_bytes=VMEM_LIMIT_BYTES)


def _proj_kernel(x_ref, nw_ref, w_ref, pa_ref, pb_ref, pc_ref, pd_ref):
    x = x_ref[...]
    h = (x * lax.rsqrt(jnp.mean(x * x, axis=-1, keepdims=True) + 1e-6) * nw_ref[...]).astype(bf16)
    off = 0
    for ref, width in ((pa_ref, PA_W), (pb_ref, PB_W), (pc_ref, PC_W), (pd_ref, PD_W)):
        ref[...] = _dot(h, w_ref[:, off:off + width])
        off += width


def _proj(x2d, t, nw, w_all, tm):
    widths = (PA_W, PB_W, PC_W, PD_W)
    return pl.pallas_call(
        _proj_kernel,
        grid=(t // tm,),
        in_specs=[pl.BlockSpec((tm, D_MODEL), lambda i: (i, 0)),
                  pl.BlockSpec((1, D_MODEL), lambda i: (0, 0)),
                  pl.BlockSpec((D_MODEL, P_ALL), lambda i: (0, 0))],
        out_specs=[pl.BlockSpec((tm, w), lambda i: (i, 0)) for w in widths],
        out_shape=[jax.ShapeDtypeStruct((t, w), f32) for w in widths],
        compiler_params=_cparams(("parallel",)),
        name="proj",
    )(x2d, nw, w_all)


def _conv4(xb_ref, rows, cw):
    y = xb_ref[:, 5:5 + rows, :] * cw[0:1, :]
    for j in range(1, 4):
        y = y + xb_ref[:, 5 + j:5 + j + rows, :] * cw[j:j + 1, :]
    return y


def _gdn_kernel(pa_ref, cb_ref, s0_ref, cw_ref, vec_ref, o_ref, sout_ref, xb_ref, s_ref, *, nseq, cps):
    c = pl.program_id(1)
    nb = nseq * cps
    span = cps * CHUNK
    rows = nb * CHUNK
    m = _masks(nb)

    @pl.when(c == 0)
    def _():
        s_ref[...] = s0_ref[...]
        xb_ref[:, 0:8, :] = cb_ref[...]

    vec = vec_ref[...]
    neg_exp_alog, dt_bias, norm_w = vec[0:1, :], vec[1:2, :], vec[2:3, :]

    xb_ref[:, 8:8 + span, :] = pa_ref[:, :, 0:3 * GW]
    y = _conv4(xb_ref, span, cw_ref[...])
    xb_ref[:, 0:8, :] = xb_ref[:, span:span + 8, :]
    y = y.reshape(rows, 3 * GW)
    cq = y * jax.nn.sigmoid(y)
    q, k, v = cq[:, 0:GW], cq[:, GW:2 * GW], cq[:, 2 * GW:3 * GW]
    z = pa_ref[:, :, 3 * GW:4 * GW].reshape(rows, GW)
    b_raw = pa_ref[:, :, 4 * GW:5 * GW].reshape(rows, GW)
    a_raw = pa_ref[:, :, 5 * GW:6 * GW].reshape(rows, GW)
    q = q * lax.rsqrt(_segsum(q * q, m) + 1e-6) * (HEAD_DIM ** -0.5)
    k = k * lax.rsqrt(_segsum(k * k, m) + 1e-6)
    beta = jax.nn.sigmoid(b_raw)
    g = neg_exp_alog * _softplus(a_raw + dt_bias)
    cum = _cumsum_t(jnp.concatenate([g, jnp.where(m["strict"], g, 0.0)], axis=1), m, nb)
    gc = cum[:, 0:GW]
    decay = jnp.exp(jnp.where(m["incl"], cum[:, GW:2 * GW], -jnp.inf))
    eg = jnp.exp(gc)
    glast = _last_row(gc, nb)
    kb = k * beta
    k_p = PREP(k)
    a = jnp.where(m["strict"], _mm_bd(PREP(kb), k_p, m, nb, NT) * decay, 0.0)
    qt_p = PREP_INV(_tri_inv_q(a, m, nb, PREP_INV))
    rhs_w = kb * eg
    rhs_u = v * beta
    w = rhs_w + _mm_bd(qt_p, PREP_INV(rhs_w), m, nb)
    u = rhs_u + _mm_bd(qt_p, PREP_INV(rhs_u), m, nb)
    qk_p = PREP(_mm_bd(PREP(q), k_p, m, nb, NT) * decay)
    qg_p = PREP(q * eg)
    kg_p = PREP(k * jnp.exp(glast - gc))
    w_p = PREP(w)
    state = [s_ref[b] for b in range(nseq)]
    o_blocks = [None] * nb
    for c2 in range(cps):
        blks = [b * cps + c2 for b in range(nseq)]
        s_p = [PREP(s) for s in state]
        vn_p = [PREP(_rs(u, k_) - _mm(_rsp(w_p, k_), s_p[b])) for b, k_ in enumerate(blks)]
        for b, k_ in enumerate(blks):
            o_blocks[k_] = _mm(_rsp(qg_p, k_), s_p[b]) + _mm(_rsp(qk_p, k_), _bd(vn_p[b], m))
        state = [state[b] * jnp.exp(_rs(gc, k_)[CHUNK - 1:CHUNK])
                 + jnp.where(m["bd"], _mm(_rsp(kg_p, k_), vn_p[b], TN), 0.0) for b, k_ in enumerate(blks)]
    for b in range(nseq):
        s_ref[b] = state[b]
    o = jnp.concatenate(o_blocks, axis=0)
    o = o * lax.rsqrt(_segsum(o * o, m) * (1.0 / HEAD_DIM) + 1e-6) * norm_w * (z * jax.nn.sigmoid(z))
    o_ref[...] = o.reshape(nseq, span, GW)

    @pl.when(c == pl.num_programs(1) - 1)
    def _():
        sout_ref[...] = s_ref[...]


def _step_shape(bsz, length):
    cps = 1
    while cps * 2 <= BLOCKS_PER_STEP // SEQ_PER_STEP and length % (cps * 2 * CHUNK) == 0:
        cps *= 2
    nseq = min(BLOCKS_PER_STEP // cps, bsz)
    assert bsz % nseq == 0
    return nseq, cps


def _gdn(pa, cb8, s0_bd, cw, vec):
    bsz, length, _ = pa.shape
    nb, cps = _step_shape(bsz, length)
    span = cps * CHUNK
    grid = (bsz // nb, length // span)
    return pl.pallas_call(
        functools.partial(_gdn_kernel, nseq=nb, cps=cps),
        grid=grid,
        in_specs=[pl.BlockSpec((nb, span, PA_W), lambda i, c: (i, c, 0)),
                  pl.BlockSpec((nb, 8, 3 * GW), lambda i, c: (i, 0, 0)),
                  pl.BlockSpec((nb, GW, GW), lambda i, c: (i, 0, 0)),
                  pl.BlockSpec((4, 3 * GW), lambda i, c: (0, 0)),
                  pl.BlockSpec((8, GW), lambda i, c: (0, 0))],
        out_specs=[pl.BlockSpec((nb, span, GW), lambda i, c: (i, c, 0)),
                   pl.BlockSpec((nb, GW, GW), lambda i, c: (i, 0, 0))],
        out_shape=[jax.ShapeDtypeStruct((bsz, length, GW), f32),
                   jax.ShapeDtypeStruct((bsz, GW, GW), f32)],
        scratch_shapes=[pltpu.VMEM((nb, span + 8, 3 * GW), f32),
                        pltpu.VMEM((nb, GW, GW), f32)],
        compiler_params=_cparams(("arbitrary", "arbitrary")),
        name="gdn",
    )(pa, cb8, s0_bd, cw, vec)


def _rwkv_kernel(pb_ref, sh_ref, s0_ref, vec_ref, mu_ref, wup_ref, aup_ref, gup_ref,
                 o_ref, sout_ref, prev_ref, s_ref, *, nseq, cps):
    c = pl.program_id(1)
    nb = nseq * cps
    span = cps * CHUNK
    rows = nb * CHUNK
    m = _masks(nb)

    @pl.when(c == 0)
    def _():
        s_ref[...] = s0_ref[...]
        prev_ref[...] = sh_ref[...]

    vec = vec_ref[...]
    w0, a0, k_k, k_a, r_k, ln_w, ln_b = (vec[i:i + 1, :] for i in range(7))

    p = pb_ref[...].reshape(rows, PB_W)
    first = (_iota((rows, PB_W), 0) & (span - 1)) == 0
    carried = jnp.concatenate([jnp.broadcast_to(prev_ref[b], (span, PB_W)) for b in range(nseq)], axis=0)
    prev = jnp.where(first, carried, pltpu.roll(p, 1, 0))
    for b in range(nseq):
        prev_ref[b] = _rs(p, b * cps + cps - 1)[CHUNK - 1:CHUNK, :]
    xs = p + (prev - p) * mu_ref[...]
    r, k, v = xs[:, 0:GW], xs[:, GW:2 * GW], xs[:, 2 * GW:3 * GW]
    wd, ad, gd = xs[:, 768:832], xs[:, 832:896], xs[:, 896:1024]
    logw = -RWKV_W_SCALE * jax.nn.sigmoid(w0 + _mm(_p3(jnp.tanh(wd)), _p3(wup_ref[...])))
    a = jax.nn.sigmoid(a0 + _mm(_p3(ad), _p3(aup_ref[...])))
    gate = _mm(_p3(jax.nn.sigmoid(gd)), _p3(gup_ref[...]))
    kk = k * k_k
    kk = kk * lax.rsqrt(_segsum(kk * kk, m) + 1e-6)
    k = k * (1.0 + (a - 1.0) * k_a)

    gc = _cumsum_t(logw, m, nb)
    glast = _last_row(gc, nb)
    eng = jnp.exp(-gc)
    kkd = kk * jnp.exp(gc - logw)
    bvec = a * kk
    rd = r * jnp.exp(gc)
    kkd_p, rd_p = PREP(kkd), PREP(rd)
    binv_p, kinv_p, v_p = PREP(bvec * eng), PREP(k * eng), PREP(v)

    def intra(b):
        lhs = tuple(jnp.concatenate([_rs(x, b), _rs(y, b)], axis=0) for x, y in zip(kkd_p, rd_p))
        return jnp.concatenate([_mm(lhs, _bd(_rsp(binv_p, b), m), NT),
                                _mm(lhs, _bd(_rsp(kinv_p, b), m), NT)], axis=1)

    prods = [intra(b) for b in range(nb)]
    top = jnp.concatenate([x[0:CHUNK] for x in prods], axis=0)
    bot = jnp.concatenate([x[CHUNK:] for x in prods], axis=0)
    amat = jnp.where(m["strict"], top[:, 0:GW], 0.0)
    bmat = jnp.where(m["strict"], top[:, GW:], 0.0)
    rb = jnp.where(m["incl"], bot[:, 0:GW], 0.0)
    rk = jnp.where(m["incl"], bot[:, GW:], 0.0)
    qt_p = PREP_INV(_tri_inv_q(amat, m, nb, PREP_INV))

    bmv = _mm_bd(PREP(bmat), v_p, m, nb)
    rkv = _mm_bd(PREP(rk), v_p, m, nb)
    rb_p = PREP(rb)
    dec_end = jnp.exp(glast - gc)
    bend_p, kend_p = PREP(bvec * dec_end), PREP(k * dec_end)
    state = [s_ref[b] for b in range(nseq)]
    o_blocks = [None] * nb
    for c2 in range(cps):
        blks = [b * cps + c2 for b in range(nseq)]
        s_p = [PREP(s) for s in state]
        rhs = [_mm(_rsp(kkd_p, k_), s_p[b], NT) + _rs(bmv, k_) for b, k_ in enumerate(blks)]
        u_p = [PREP(-(rhs[b] + _mm(_rsp(qt_p, k_), _bd(PREP_INV(rhs[b]), m))))
               for b, k_ in enumerate(blks)]
        for b, k_ in enumerate(blks):
            o_blocks[k_] = (_mm(_rsp(rd_p, k_), s_p[b], NT) + _mm(_rsp(rb_p, k_), _bd(u_p[b], m))
                            + _rs(rkv, k_))
        new_state = []
        for b, k_ in enumerate(blks):
            lhs = tuple(jnp.concatenate([x, _rs(y, k_)], axis=0) for x, y in zip(u_p[b], v_p))
            rhs2 = tuple(jnp.concatenate([_rs(x, k_), _rs(y, k_)], axis=0) for x, y in zip(bend_p, kend_p))
            new_state.append(state[b] * jnp.exp(_rs(gc, k_)[CHUNK - 1:CHUNK])
                             + jnp.where(m["bd"], _mm(lhs, rhs2, TN), 0.0))
        state = new_state
    for b in range(nseq):
        s_ref[b] = state[b]
    o = jnp.concatenate(o_blocks, axis=0)

    mean = _segsum(o, m) * (1.0 / HEAD_DIM)
    oc = o - mean
    var = _segsum(oc * oc, m) * (1.0 / HEAD_DIM)
    on = oc * lax.rsqrt(var + RWKV_LN_EPS) * ln_w + ln_b
    bonus = _segsum(r * k * r_k, m) * v
    o_ref[...] = ((on + bonus) * gate).reshape(nseq, span, GW)

    @pl.when(c == pl.num_programs(1) - 1)
    def _():
        sout_ref[...] = s_ref[...]


def _rwkv(pb, shift, s0_bd, vec, mu, wup, aup, gup):
    bsz, length, _ = pb.shape
    nb, cps = _step_shape(bsz, length)
    span = cps * CHUNK
    grid = (bsz // nb, length // span)
    full = lambda shape: pl.BlockSpec(shape, lambda i, c: tuple(0 for _ in shape))
    return pl.pallas_call(
        functools.partial(_rwkv_kernel, nseq=nb, cps=cps),
        grid=grid,
        in_specs=[pl.BlockSpec((nb, span, PB_W), lambda i, c: (i, c, 0)),
                  pl.BlockSpec((nb, 1, PB_W), lambda i, c: (i, 0, 0)),
                  pl.BlockSpec((nb, GW, GW), lambda i, c: (i, 0, 0)),
                  full((8, GW)), full((1, PB_W)), full((64, GW)), full((64, GW)), full((128, GW))],
        out_specs=[pl.BlockSpec((nb, span, GW), lambda i, c: (i, c, 0)),
                   pl.BlockSpec((nb, GW, GW), lambda i, c: (i, 0, 0))],
        out_shape=[jax.ShapeDtypeStruct((bsz, length, GW), f32),
                   jax.ShapeDtypeStruct((bsz, GW, GW), f32)],
        scratch_shapes=[pltpu.VMEM((nb, 1, PB_W), f32),
                        pltpu.VMEM((nb, GW, GW), f32)],
        compiler_params=_cparams(("arbitrary", "arbitrary")),
        name="rwkv",
    )(pb, shift, s0_bd, vec, mu, wup, aup, gup)


def _neg_expm1(x):
    u = jnp.exp(x)
    um1 = u - 1.0
    lu = jnp.log(u)
    safe = jnp.where(um1 == 0.0, x, um1 * x / jnp.where(lu == 0.0, 1.0, lu))
    return -jnp.where(x < -0.5, um1, safe)


def _lru_kernel(pc_ref, cb_ref, h0_ref, cw_ref, vec_ref, wa_ref, wx_ref,
                o_ref, hout_ref, xb_ref, h_ref, *, nb, rows):
    c = pl.program_id(1)

    @pl.when(c == 0)
    def _():
        h_ref[...] = h0_ref[...]
        xb_ref[:, 0:8, :] = cb_ref[...]

    cw = cw_ref[...]
    vec = vec_ref[...]
    conv_b, ba, bx, sp_neg_lam = (vec[i:i + 1, :] for i in range(4))
    t = _iota((rows, GW), 0)

    xb_ref[:, 8:8 + rows, :] = pc_ref[:, :, 0:GW]
    xc_all = _conv4(xb_ref, rows, cw) + conv_b
    xb_ref[:, 0:8, :] = xb_ref[:, rows:rows + 8, :]
    for b in range(nb):
        xc = xc_all[b]
        gb = pc_ref[b, :, GW:2 * GW]
        xcp = PREP(xc)
        r = jax.nn.sigmoid(_mm(xcp, (wa_ref[...],)) + ba)
        i = jax.nn.sigmoid(_mm(xcp, (wx_ref[...],)) + bx)
        log_a = -LRU_C * r * sp_neg_lam
        a = jnp.exp(log_a)
        bt = jnp.sqrt(_neg_expm1(2.0 * log_a)) * (i * xc)
        d = 1
        while d < rows:
            keep = t >= d
            a_sh = jnp.where(keep, pltpu.roll(a, d, 0), 1.0)
            b_sh = jnp.where(keep, pltpu.roll(bt, d, 0), 0.0)
            bt = a * b_sh + bt
            a = a * a_sh
            d *= 2
        h = bt + a * h_ref[b]
        h_ref[b] = h[rows - 1:rows, :]
        o_ref[b] = h * jax.nn.gelu(gb)

    @pl.when(c == pl.num_programs(1) - 1)
    def _():
        hout_ref[...] = h_ref[...]


def _lru(pc, cb8, h0, cw, vec, wa_bd, wx_bd):
    bsz, length, _ = pc.shape
    nb = SEQ_PER_STEP
    rows = min(length, 256)
    grid = (bsz // nb, length // rows)
    full = lambda shape: pl.BlockSpec(shape, lambda i, c: tuple(0 for _ in shape))
    return pl.pallas_call(
        functools.partial(_lru_kernel, nb=nb, rows=rows),
        grid=grid,
        in_specs=[pl.BlockSpec((nb, rows, PC_W), lambda i, c: (i, c, 0)),
                  pl.BlockSpec((nb, 8, GW), lambda i, c: (i, 0, 0)),
                  pl.BlockSpec((nb, 1, GW), lambda i, c: (i, 0, 0)),
                  full((4, GW)), full((8, GW)), full((GW, GW)), full((GW, GW))],
        out_specs=[pl.BlockSpec((nb, rows, GW), lambda i, c: (i, c, 0)),
                   pl.BlockSpec((nb, 1, GW), lambda i, c: (i, 0, 0))],
        out_shape=[jax.ShapeDtypeStruct((bsz, length, GW), f32),
                   jax.ShapeDtypeStruct((bsz, 1, GW), f32)],
        scratch_shapes=[pltpu.VMEM((nb, rows + 8, GW), f32),
                        pltpu.VMEM((nb, 1, GW), f32)],
        compiler_params=_cparams(("arbitrary", "arbitrary")),
        name="lru",
    )(pc, cb8, h0, cw, vec, wa_bd, wx_bd)


def _ret_kernel(pd_ref, r0_ref, cos_ref, sin_ref, tab_ref, vec_ref, o_ref, rout_ref, s_ref, *, nseq, cps):
    c = pl.program_id(1)
    nb = nseq * cps
    span = cps * CHUNK
    rows = nb * CHUNK
    m = _masks(nb)

    @pl.when(c == 0)
    def _():
        s_ref[...] = r0_ref[...]

    tile = lambda x, n: jnp.concatenate([x] * n, axis=0)
    cos, sin = tile(cos_ref[...], nseq), tile(sin_ref[...], nseq)
    dec, qgs, kgs = tile(tab_ref[0], nb), tile(tab_ref[1], nb), tile(tab_ref[2], nb)
    vec = vec_ref[...]
    gl, norm_w, norm_b = vec[0:1, :], vec[1:2, :], vec[2:3, :]
    low_half = m["s"] < (HEAD_DIM // 2)

    def rot(x):
        swapped = jnp.where(low_half, pltpu.roll(x, GW - HEAD_DIM // 2, 1), pltpu.roll(x, HEAD_DIM // 2, 1))
        return x * cos + swapped * sin

    q = rot(pd_ref[:, :, 0:GW].reshape(rows, GW))
    k = rot(pd_ref[:, :, GW:2 * GW].reshape(rows, GW)) * (HEAD_DIM ** -0.5)
    v = pd_ref[:, :, 2 * GW:3 * GW].reshape(rows, GW)
    gate = pd_ref[:, :, 3 * GW:4 * GW].reshape(rows, GW)
    v_p = PREP(v)
    qk = _mm_bd(PREP(q), PREP(k), m, nb, NT) * dec
    qg_p, kg_p = PREP(q * qgs), PREP(k * kgs)
    intra = _mm_bd(PREP(qk), v_p, m, nb)
    state = [s_ref[b] for b in range(nseq)]
    o_blocks = [None] * nb
    for c2 in range(cps):
        for b in range(nseq):
            k_ = b * cps + c2
            o_blocks[k_] = _mm(_rsp(qg_p, k_), PREP(state[b])) + _rs(intra, k_)
            state[b] = state[b] * gl + jnp.where(m["bd"], _mm(_rsp(kg_p, k_), _rsp(v_p, k_), TN), 0.0)
    for b in range(nseq):
        s_ref[b] = state[b]
    o = jnp.concatenate(o_blocks, axis=0)
    mean = _segsum(o, m) * (1.0 / HEAD_DIM)
    oc = o - mean
    var = _segsum(oc * oc, m) * (1.0 / HEAD_DIM)
    on = oc * lax.rsqrt(var + 1e-5) * norm_w + norm_b
    o_ref[...] = (gate * jax.nn.sigmoid(gate) * on).reshape(nseq, span, GW)

    @pl.when(c == pl.num_programs(1) - 1)
    def _():
        rout_ref[...] = s_ref[...]


def _ret(pd, r0_bd, cos, sin, tab, vec):
    bsz, length, _ = pd.shape
    nb, cps = _step_shape(bsz, length)
    span = cps * CHUNK
    grid = (bsz // nb, length // span)
    full = lambda shape: pl.BlockSpec(shape, lambda i, c: tuple(0 for _ in shape))
    return pl.pallas_call(
        functools.partial(_ret_kernel, nseq=nb, cps=cps),
        grid=grid,
        in_specs=[pl.BlockSpec((nb, span, PD_W), lambda i, c: (i, c, 0)),
                  pl.BlockSpec((nb, GW, GW), lambda i, c: (i, 0, 0)),
                  pl.BlockSpec((span, GW), lambda i, c: (c, 0)),
                  pl.BlockSpec((span, GW), lambda i, c: (c, 0)),
                  full((3, CHUNK, GW)), full((8, GW))],
        out_specs=[pl.BlockSpec((nb, span, GW), lambda i, c: (i, c, 0)),
                   pl.BlockSpec((nb, GW, GW), lambda i, c: (i, 0, 0))],
        out_shape=[jax.ShapeDtypeStruct((bsz, length, GW), f32),
                   jax.ShapeDtypeStruct((bsz, GW, GW), f32)],
        scratch_shapes=[pltpu.VMEM((nb, GW, GW), f32)],
        compiler_params=_cparams(("arbitrary", "arbitrary")),
        name="ret",
    )(pd, r0_bd, cos, sin, tab, vec)


def _post_kernel(x_ref, oa_ref, ob_ref, oc_ref, od_ref, wout_ref, nw_ref, wr_ref, br_ref, x1g_ref):
    acc = x_ref[...]
    for j, ref in enumerate((oa_ref, ob_ref, oc_ref, od_ref)):
        acc = acc + _dot(ref[...].astype(bf16), wout_ref[j * GW:(j + 1) * GW, :])
    x1g_ref[:, 0:D_MODEL] = acc
    tn = acc * lax.rsqrt(jnp.mean(acc * acc, axis=-1, keepdims=True) + 1e-6) * nw_ref[...]
    logits = _mm(_p3(tn), _p3(wr_ref[...])) + br_ref[...]
    lane = _iota(logits.shape, 1)
    lane_f = lane.astype(f32)
    ninf = -jnp.inf
    is_g = (lane >= N_EXPERTS) & (lane < N_EXPERTS + N_GROUPS)
    lg = jnp.where(is_g, logits, ninf)
    gmax = jnp.max(lg, axis=-1, keepdims=True)
    grp_f = jnp.min(jnp.where(lg == gmax, lane_f, 1e9), axis=-1, keepdims=True) - N_EXPERTS
    p_grp = 1.0 / jnp.sum(jnp.exp(lg - gmax), axis=-1, keepdims=True)
    in_grp = (lane >> 2) == grp_f.astype(jnp.int32)
    le = jnp.where(in_grp, logits, ninf)
    v1 = jnp.max(le, axis=-1, keepdims=True)
    i1 = jnp.min(jnp.where(le == v1, lane_f, 1e9), axis=-1, keepdims=True)
    le2 = jnp.where(lane_f == i1, ninf, le)
    v2 = jnp.max(le2, axis=-1, keepdims=True)
    i2 = jnp.min(jnp.where(le2 == v2, lane_f, 1e9), axis=-1, keepdims=True)
    e2 = jnp.exp(v2 - v1)
    wt1 = p_grp / (1.0 + e2)
    wt2 = p_grp * e2 / (1.0 + e2)
    gates = jnp.where(lane_f == i1, wt1, 0.0) + jnp.where(lane_f == i2, wt2, 0.0)
    x1g_ref[:, D_MODEL:] = jnp.where(lane == GRP_LANE, grp_f, gates)


def _post(x2d, outs, wout, nw, wr, br, tm):
    t = outs[0].shape[0]
    row = lambda w: pl.BlockSpec((tm, w), lambda i: (i, 0))
    full = lambda shape: pl.BlockSpec(shape, lambda i: tuple(0 for _ in shape))
    return pl.pallas_call(
        _post_kernel,
        grid=(t // tm,),
        in_specs=[row(D_MODEL), row(GW), row(GW), row(GW), row(GW),
                  full((D_MODEL, D_MODEL)), full((1, D_MODEL)), full((D_MODEL, 128)), full((1, 128))],
        out_specs=row(XG_W),
        out_shape=jax.ShapeDtypeStruct((t, XG_W), f32),
        compiler_params=_cparams(("parallel",)),
        name="post",
    )(x2d, *outs, wout, nw, wr, br)


ROW_DMA_UNROLL = 8


def _route(x1g, tm):
    t = x1g.shape[0]
    n_tiles = t // tm
    n_items = n_tiles + N_GROUPS - 1
    grp = x1g[:, D_MODEL + GRP_LANE].astype(jnp.int32)
    shift = max((t - 1).bit_length(), 1)
    order = jnp.sort((grp << shift) | jnp.arange(t, dtype=jnp.int32)) & ((1 << shift) - 1)
    counts = jnp.sum((grp[:, None] == jnp.arange(N_GROUPS, dtype=jnp.int32)[None, :]).astype(jnp.int32), axis=0)
    inner_ends = jnp.cumsum(counts)[:-1]
    tile_start = jnp.arange(n_tiles, dtype=jnp.int32) * tm
    g_lo = jnp.sum((tile_start[:, None] >= inner_ends[None, :]).astype(jnp.int32), axis=1)
    g_hi = jnp.sum(((tile_start + tm - 1)[:, None] >= inner_ends[None, :]).astype(jnp.int32), axis=1)
    visits = g_hi - g_lo + 1
    item_end = jnp.cumsum(visits)
    w = jnp.arange(n_items, dtype=jnp.int32)
    valid = w < item_end[-1]
    w_tile = jnp.minimum(jnp.sum((w[:, None] >= item_end[None, :]).astype(jnp.int32), axis=1), n_tiles - 1)
    within = w - (item_end - visits)[w_tile]
    w_gid = jnp.where(valid, g_lo[w_tile] + within, g_hi[n_tiles - 1])
    first = valid & (within == 0)
    last = valid & (within == visits[w_tile] - 1)
    i32 = lambda a: a.astype(jnp.int32)
    return order.reshape(n_tiles, 1, tm), i32(w_tile), i32(w_gid), i32(first), i32(last), i32(valid)


def _moe_kernel(tile_ref, gid_ref, first_ref, last_ref, valid_ref, idx_ref, idx_next_ref, x1g_ref,
                wg_ref, wu_ref, wd_ref, nffn_ref, nf_ref, x2_ref, xbuf, acc_ref, obuf, gsem, ssem,
                *, final_norm, tm, n_tiles):
    w = pl.program_id(0)
    tile = tile_ref[w]
    slot = tile % 2
    is_first = first_ref[w] == 1
    is_last = last_ref[w] == 1

    def gather_copy(ref, g, j, s):
        return pltpu.make_async_copy(x1g_ref.at[pl.ds(ref[0, 0, g * 8 + j], 1)],
                                     xbuf.at[s, g, pl.ds(j, 1)], gsem.at[s])

    def scatter_copy(g, j):
        return pltpu.make_async_copy(obuf.at[g, pl.ds(j, 1)],
                                     x2_ref.at[pl.ds(idx_ref[0, 0, g * 8 + j], 1)], ssem)

    def each_row(fn):
        def body(g, carry):
            for j in range(8):
                fn(g, j)
            return carry
        lax.fori_loop(0, tm // 8, body, 0, unroll=ROW_DMA_UNROLL // 8)

    @pl.when(w == 0)
    def _():
        each_row(lambda g, j: gather_copy(idx_ref, g, j, 0).start())

    @pl.when(is_first & (tile + 1 < n_tiles))
    def _():
        each_row(lambda g, j: gather_copy(idx_next_ref, g, j, 1 - slot).start())

    @pl.when(is_first)
    def _():
        each_row(lambda g, j: gather_copy(idx_ref, g, j, slot).wait())
        acc_ref[...] = xbuf[slot].reshape(tm, XG_W)[:, 0:D_MODEL]

    @pl.when(valid_ref[w] == 1)
    def _():
        rows = xbuf[slot].reshape(tm, XG_W)
        x1 = rows[:, 0:D_MODEL]
        gates = rows[:, D_MODEL:]
        tb = (x1 * lax.rsqrt(jnp.mean(x1 * x1, axis=-1, keepdims=True) + 1e-6) * nffn_ref[...]).astype(bf16)
        lane = _iota(gates.shape, 1)
        first_expert = gid_ref[w] * EXPERTS_PER_GROUP
        acc = acc_ref[...]
        for j in range(EXPERTS_PER_GROUP):
            hg = _dot(tb, wg_ref[j])
            hu = _dot(tb, wu_ref[j])
            he = (hg * jax.nn.sigmoid(hg) * hu).astype(bf16)
            ge = jnp.sum(jnp.where(lane == first_expert + j, gates, 0.0), axis=-1, keepdims=True)
            acc = acc + ge * _dot(he, wd_ref[j])
        acc_ref[...] = acc

    @pl.when(is_last)
    def _():
        @pl.when(tile > 0)
        def _():
            each_row(lambda g, j: scatter_copy(g, j).wait())

        out = acc_ref[...]
        if final_norm:
            out = out * lax.rsqrt(jnp.mean(out * out, axis=-1, keepdims=True) + 1e-6) * nf_ref[...]
        obuf[...] = out.reshape(obuf.shape)
        each_row(lambda g, j: scatter_copy(g, j).start())

    @pl.when(w == pl.num_programs(0) - 1)
    def _():
        each_row(lambda g, j: scatter_copy(g, j).wait())


def _moe(x1g, order, w_tile, w_gid, w_first, w_last, w_valid, wg, wu, wd, nffn, nf, tm, final_norm):
    t = x1g.shape[0]
    n_tiles = order.shape[0]
    epg = EXPERTS_PER_GROUP
    smem_tile = lambda fn: pl.BlockSpec((1, 1, tm), fn, memory_space=pltpu.SMEM)
    wspec = lambda shape: pl.BlockSpec(shape, lambda w, tile, gid, *_: (gid[w], 0, 0))
    vec = pl.BlockSpec((1, D_MODEL), lambda w, *_: (0, 0))
    grid_spec = pltpu.PrefetchScalarGridSpec(
        num_scalar_prefetch=5,
        grid=(w_tile.shape[0],),
        in_specs=[smem_tile(lambda w, tile, *_: (tile[w], 0, 0)),
                  smem_tile(lambda w, tile, *_: (jnp.minimum(tile[w] + 1, n_tiles - 1), 0, 0)),
                  pl.BlockSpec(memory_space=pl.ANY),
                  wspec((epg, D_MODEL, D_EXPERT)), wspec((epg, D_MODEL, D_EXPERT)),
                  wspec((epg, D_EXPERT, D_MODEL)), vec, vec],
        out_specs=pl.BlockSpec(memory_space=pl.ANY),
        scratch_shapes=[pltpu.VMEM((2, tm // 8, 8, XG_W), f32), pltpu.VMEM((tm, D_MODEL), f32),
                        pltpu.VMEM((tm // 8, 8, D_MODEL), f32),
                        pltpu.SemaphoreType.DMA((2,)), pltpu.SemaphoreType.DMA],
    )
    return pl.pallas_call(
        functools.partial(_moe_kernel, final_norm=final_norm, tm=tm, n_tiles=n_tiles),
        grid_spec=grid_spec,
        out_shape=jax.ShapeDtypeStruct((t, D_MODEL), f32),
        compiler_params=_cparams(("arbitrary",)),
        name="moe",
    )(w_tile, w_gid, w_first, w_last, w_valid, order, order, x1g, wg, wu, wd, nffn, nf)


def _to_bd(s):
    bsz = s.shape[0]
    eye = jnp.eye(N_HEADS, dtype=s.dtype)
    return jnp.einsum("bhij,hg->bhigj", s, eye).reshape(bsz, GW, GW)


def _from_bd(s):
    bsz = s.shape[0]
    s5 = s.reshape(bsz, N_HEADS, HEAD_DIM, N_HEADS, HEAD_DIM)
    return jnp.stack([s5[:, h, :, h, :] for h in range(N_HEADS)], axis=1)


def _pad_hist(buf):
    return jnp.pad(buf, ((0, 0), (5, 0), (0, 0)))


def _rows8(*rows):
    out = [jnp.reshape(r, (1, -1)).astype(f32) for r in rows]
    width = out[0].shape[1]
    out += [jnp.zeros((1, width), f32)] * (8 - len(out))
    return jnp.concatenate(out, axis=0)


def _rep_head(x):
    return jnp.repeat(x, HEAD_DIM)


def _block_diag4(w):
    eye = jnp.eye(N_HEADS, dtype=w.dtype)
    return jnp.einsum("hij,hg->higj", w, eye).reshape(GW, GW)


def _rope_tables(pos):
    half = HEAD_DIM // 2
    inv = ROPE_BASE ** (-jnp.arange(half, dtype=f32) / half)
    ang = pos.astype(f32)[:, None] * inv
    cos, sin = jnp.cos(ang), jnp.sin(ang)
    cos_h = jnp.concatenate([cos, cos], axis=-1)
    sin_h = jnp.concatenate([-sin, sin], axis=-1)
    return jnp.tile(cos_h, (1, N_HEADS)), jnp.tile(sin_h, (1, N_HEADS))


def _ret_tables():
    log_gamma = jnp.log1p(-jnp.exp2(-5.0 - jnp.arange(N_HEADS, dtype=f32)))
    lg = _rep_head(log_gamma)[None, :]
    g = jnp.cumsum(jnp.broadcast_to(lg, (CHUNK, GW)), axis=0)
    g_h = g[:, ::HEAD_DIM]
    diff = g_h[:, None, :] - g_h[None, :, :]
    incl = jnp.tril(jnp.ones((CHUNK, CHUNK), dtype=bool))[:, :, None]
    dec = jnp.exp(jnp.where(incl, diff, -jnp.inf))
    dec = jnp.transpose(dec, (0, 2, 1)).reshape(CHUNK, GW)
    qgs = jnp.exp(g)
    kgs = jnp.exp(g[-1:] - g)
    gl = jnp.exp(g[-1:])
    return jnp.stack([dec, qgs, kgs]), gl


def _layer_params(l, p):
    f = lambda a: a[l]
    w_in = f(p["w_in"])
    c = [0, 768, 1024, 1028, 1032, 2056, 2312, 2568, 3336, 3592]
    a_qkv, a_z, a_b, a_a, b_rw, c_x, c_g, d_qkv, d_g = (w_in[:, c[i]:c[i + 1]] for i in range(9))
    rep = lambda w: jnp.repeat(w, HEAD_DIM, axis=1)
    w_all = jnp.concatenate([a_qkv, a_z, rep(a_b), rep(a_a), b_rw, c_x, c_g, d_qkv, d_g], axis=1).astype(bf16)
    lp = dict(
        norm_mix_w=f(p["norm_mix_w"])[None, :],
        w_all=w_all,
        gdn_cw=f(p["gdn_conv_w"]),
        gdn_vec=_rows8(-jnp.exp(_rep_head(f(p["gdn_a_log"]))), _rep_head(f(p["gdn_dt_bias"])),
                       jnp.tile(f(p["gdn_norm_w"]), N_HEADS)),
        rwkv_vec=_rows8(f(p["rwkv_w0"]), f(p["rwkv_a0"]), f(p["rwkv_k_k"]), f(p["rwkv_k_a"]),
                        f(p["rwkv_r_k"]).reshape(-1), f(p["rwkv_ln_w"]), f(p["rwkv_ln_b"])),
        rwkv_mu=f(p["rwkv_mu"])[None, :],
        rwkv_wup=f(p["rwkv_w_up"]), rwkv_aup=f(p["rwkv_a_up"]), rwkv_gup=f(p["rwkv_g_up"]),
        lru_cw=f(p["lru_conv_w"]),
        lru_vec=_rows8(f(p["lru_conv_b"]), f(p["lru_ba"]), f(p["lru_bx"]),
                       jax.nn.softplus(-f(p["lru_lambda"]))),
        lru_wa=_block_diag4(f(p["lru_wa"])).astype(bf16),
        lru_wx=_block_diag4(f(p["lru_wx"])).astype(bf16),
        ret_norm=(f(p["ret_norm_w"]), f(p["ret_norm_b"])),
        w_out=f(p["w_out"]).astype(bf16),
        norm_ffn_w=f(p["norm_ffn_w"])[None, :],
        w_router=jnp.concatenate([f(p["moe_router_e"]), f(p["moe_router_g"]),
                                  jnp.zeros((D_MODEL, 128 - N_EXPERTS - N_GROUPS), f32)], axis=1),
        b_router=jnp.concatenate([f(p["moe_router_e_b"]), f(p["moe_router_g_b"]),
                                  jnp.zeros((128 - N_EXPERTS - N_GROUPS,), f32)])[None, :],
        moe_wg=f(p["moe_w_gate"]).astype(bf16),
        moe_wu=f(p["moe_w_up"]).astype(bf16),
        moe_wd=f(p["moe_w_down"]).astype(bf16),
    )
    return lp


def _trunk(x, states, pos, layer_ps, norm_final_w):
    bsz, length, _ = x.shape
    t = bsz * length
    tm = min(512, t)
    assert t % tm == 0 and bsz % SEQ_PER_STEP == 0 and length % CHUNK == 0
    cos, sin = _rope_tables(pos)
    ret_tab, ret_gl = _ret_tables()
    new = {k: [] for k in ("gdn", "gdn_conv", "rwkv", "rwkv_shift", "lru", "lru_conv", "ret")}
    x2d = x.reshape(t, D_MODEL)
    for l, lp in enumerate(layer_ps):
        pa, pb, pc, pd = _proj(x2d, t, lp["norm_mix_w"], lp["w_all"], tm)
        pa = pa.reshape(bsz, length, PA_W)
        pb = pb.reshape(bsz, length, PB_W)
        pc = pc.reshape(bsz, length, PC_W)
        pd = pd.reshape(bsz, length, PD_W)
        o_a, s_gdn = _gdn(pa, _pad_hist(states["gdn_conv"][l]), _to_bd(states["gdn"][l]),
                          lp["gdn_cw"], lp["gdn_vec"])
        o_b, s_rwkv = _rwkv(pb, states["rwkv_shift"][l], _to_bd(states["rwkv"][l]), lp["rwkv_vec"],
                            lp["rwkv_mu"], lp["rwkv_wup"], lp["rwkv_aup"], lp["rwkv_gup"])
        o_c, h_lru = _lru(pc, _pad_hist(states["lru_conv"][l]), states["lru"][l][:, None, :],
                          lp["lru_cw"], lp["lru_vec"], lp["lru_wa"], lp["lru_wx"])
        o_d, s_ret = _ret(pd, _to_bd(states["ret"][l]), cos, sin, ret_tab,
                          _rows8(ret_gl, lp["ret_norm"][0], lp["ret_norm"][1]))
        outs = [o.reshape(t, GW) for o in (o_a, o_b, o_c, o_d)]
        x1g = _post(x2d, outs, lp["w_out"], lp["norm_ffn_w"], lp["w_router"], lp["b_router"], tm)
        x2d = _moe(x1g, *_route(x1g, tm), lp["moe_wg"], lp["moe_wu"], lp["moe_wd"], lp["norm_ffn_w"],
                   norm_final_w[None, :], tm, final_norm=(l == len(layer_ps) - 1))
        new["gdn"].append(_from_bd(s_gdn))
        new["gdn_conv"].append(pa[:, length - 3:, 0:3 * GW])
        new["rwkv"].append(_from_bd(s_rwkv))
        new["rwkv_shift"].append(pb[:, length - 1:, :])
        new["lru"].append(h_lru[:, 0, :])
        new["lru_conv"].append(pc[:, length - 3:, 0:GW])
        new["ret"].append(_from_bd(s_ret))
    return x2d.reshape(bsz, length, D_MODEL), {k: jnp.stack(v) for k, v in new.items()}


def _zero_states(bsz, dtype):
    return {"gdn": jnp.zeros((DEPTH, bsz, N_HEADS, HEAD_DIM, HEAD_DIM), dtype),
            "gdn_conv": jnp.zeros((DEPTH, bsz, 3, 3 * GW), dtype),
            "rwkv": jnp.zeros((DEPTH, bsz, N_HEADS, HEAD_DIM, HEAD_DIM), dtype),
            "rwkv_shift": jnp.zeros((DEPTH, bsz, 1, PB_W), dtype),
            "lru": jnp.zeros((DEPTH, bsz, GW), dtype),
            "lru_conv": jnp.zeros((DEPTH, bsz, 3, GW), dtype),
            "ret": jnp.zeros((DEPTH, bsz, N_HEADS, HEAD_DIM, HEAD_DIM), dtype)}


def kernel(x_prompt, x_sample, state_gdn, state_gdn_conv, state_rwkv, state_rwkv_shift, state_lru, state_lru_conv, state_ret, norm_mix_w, w_in, gdn_conv_w, gdn_a_log, gdn_dt_bias, gdn_norm_w, rwkv_mu, rwkv_w0, rwkv_w_up, rwkv_a0, rwkv_a_up, rwkv_g_up, rwkv_k_k, rwkv_k_a, rwkv_r_k, rwkv_ln_w, rwkv_ln_b, lru_conv_w, lru_conv_b, lru_wa, lru_ba, lru_wx, lru_bx, lru_lambda, ret_norm_w, ret_norm_b, w_out, norm_ffn_w, moe_router_g, moe_router_g_b, moe_router_e, moe_router_e_b, moe_w_gate, moe_w_up, moe_w_down, norm_final_w):
    p = dict(norm_mix_w=norm_mix_w, w_in=w_in, gdn_conv_w=gdn_conv_w, gdn_a_log=gdn_a_log,
             gdn_dt_bias=gdn_dt_bias, gdn_norm_w=gdn_norm_w, rwkv_mu=rwkv_mu, rwkv_w0=rwkv_w0,
             rwkv_w_up=rwkv_w_up, rwkv_a0=rwkv_a0, rwkv_a_up=rwkv_a_up, rwkv_g_up=rwkv_g_up,
             rwkv_k_k=rwkv_k_k, rwkv_k_a=rwkv_k_a, rwkv_r_k=rwkv_r_k, rwkv_ln_w=rwkv_ln_w,
             rwkv_ln_b=rwkv_ln_b, lru_conv_w=lru_conv_w, lru_conv_b=lru_conv_b, lru_wa=lru_wa,
             lru_ba=lru_ba, lru_wx=lru_wx, lru_bx=lru_bx, lru_lambda=lru_lambda,
             ret_norm_w=ret_norm_w, ret_norm_b=ret_norm_b, w_out=w_out, norm_ffn_w=norm_ffn_w,
             moe_router_g=moe_router_g, moe_router_g_b=moe_router_g_b, moe_router_e=moe_router_e,
             moe_router_e_b=moe_router_e_b, moe_w_gate=moe_w_gate, moe_w_up=moe_w_up,
             moe_w_down=moe_w_down)
    depth = w_in.shape[0]
    layer_ps = [_layer_params(l, p) for l in range(depth)]
    pos_p = jnp.arange(x_prompt.shape[1], dtype=jnp.int32)
    y_p, new_p = _trunk(x_prompt, _zero_states(x_prompt.shape[0], x_prompt.dtype), pos_p, layer_ps, norm_final_w)
    states_s = {"gdn": state_gdn, "gdn_conv": state_gdn_conv, "rwkv": state_rwkv,
                "rwkv_shift": state_rwkv_shift, "lru": state_lru, "lru_conv": state_lru_conv,
                "ret": state_ret}
    pos_s = PAST_LEN + jnp.arange(x_sample.shape[1], dtype=jnp.int32)
    y_s, new_s = _trunk(x_sample, states_s, pos_s, layer_ps, norm_final_w)
    return (y_p, y_s,
            new_p["gdn"], new_s["gdn"], new_p["gdn_conv"], new_s["gdn_conv"],
            new_p["rwkv"], new_s["rwkv"], new_p["rwkv_shift"], new_s["rwkv_shift"],
            new_p["lru"], new_s["lru"], new_p["lru_conv"], new_s["lru_conv"],
            new_p["ret"], new_s["ret"])
```
